```python
import jax
import jax.numpy as jnp
from jax import lax
import numpy as np

D_MODEL = 2048
BATCH = 8
SEQ = 4096
DEPTH = 4

HEAD_DIM = 128
BRANCH_WIDTH = D_MODEL // 2
N_BRANCH = 3
PLE_DIM = 256
NORM_EPS = 1e-6

LRU_BLOCK = 64
LRU_BLOCKS = BRANCH_WIDTH // LRU_BLOCK
CONV_WIDTH = 4
LRU_C = 8.0

NSA_HEADS = BRANCH_WIDTH // HEAD_DIM
NSA_KV_HEADS = 2
NSA_GROUP = NSA_HEADS // NSA_KV_HEADS
NSA_KV_WIDTH = NSA_KV_HEADS * HEAD_DIM
CMP_BLOCK = 32
CMP_STRIDE = 16
CMP_HIDDEN = 256
SEL_BLOCK = 64
SEL_TOPK = 16
WINDOW = 512
NSA_Q_BLOCK = 32

FOX_HEADS = BRANCH_WIDTH // HEAD_DIM
FOX_Q_BLOCK = 128
FORGET_BIAS = 3.0

IN_SPLITS = (
    ('lru_x', BRANCH_WIDTH), ('lru_gate', BRANCH_WIDTH),
    ('nsa_q', BRANCH_WIDTH),
    ('nsa_k_cmp', NSA_KV_WIDTH), ('nsa_v_cmp', NSA_KV_WIDTH),
    ('nsa_k_slc', NSA_KV_WIDTH), ('nsa_v_slc', NSA_KV_WIDTH),
    ('nsa_k_win', NSA_KV_WIDTH), ('nsa_v_win', NSA_KV_WIDTH),
    ('nsa_bgate', NSA_HEADS * 3), ('nsa_gate', BRANCH_WIDTH),
    ('fox_q', BRANCH_WIDTH), ('fox_k', BRANCH_WIDTH), ('fox_v', BRANCH_WIDTH),
    ('fox_f', FOX_HEADS), ('fox_gate', BRANCH_WIDTH),
    ('merge', N_BRANCH * D_MODEL),
)
N_IN = sum(size for _, size in IN_SPLITS)

kernel_name = 'hybrid_rglru_nsa_fox_block'


def column_offsets():
    offs, start = {}, 0
    for name, size in IN_SPLITS:
        offs[name] = (start, size)
        start += size
    return offs


def split_columns(z):
    return {name: z[..., s:s + n] for name, (s, n) in column_offsets().items()}


def split_heads(t, n_heads):
    b, s, _ = t.shape
    return t.reshape(b, s, n_heads, HEAD_DIM)


def rms_norm(x, gain):
    xf = x.astype(jnp.float32)
    y = xf * lax.rsqrt(jnp.mean(xf * xf, axis=-1, keepdims=True) + NORM_EPS)
    return (y * gain.astype(jnp.float32)).astype(x.dtype)


def masked_softmax(logits, mask):
    logits = jnp.where(mask, logits.astype(jnp.float32), -jnp.inf)
    m = jnp.max(logits, axis=-1, keepdims=True)
    m = jnp.where(jnp.isfinite(m), m, 0.0)
    e = jnp.exp(logits - m)
    return e / jnp.maximum(jnp.sum(e, axis=-1, keepdims=True), 1e-30)


def linear_combine(left, right):
    a_l, b_l = left
    a_r, b_r = right
    return a_l * a_r, a_r * b_l + b_r


def rglru_branch(u, conv_w, conv_b, wa, ba, wx, bx, lam):
    b, s, w = u.shape
    taps = conv_w[:, None, :].astype(u.dtype)
    uc = lax.conv_general_dilated(u, taps, (1,), ((CONV_WIDTH - 1, 0),),
                                  dimension_numbers=('NWC', 'WIO', 'NWC'),
                                  feature_group_count=w) + conv_b
    ub = uc.reshape(b, s, LRU_BLOCKS, LRU_BLOCK)
    rec = jax.nn.sigmoid((jnp.einsum('bsnk,nkj->bsnj', ub, wa).reshape(b, s, w) + ba).astype(jnp.float32))
    inp = jax.nn.sigmoid((jnp.einsum('bsnk,nkj->bsnj', ub, wx).reshape(b, s, w) + bx).astype(jnp.float32))
    log_a = -LRU_C * rec * jax.nn.softplus(-lam.astype(jnp.float32))
    a = jnp.exp(log_a)
    drive = jnp.sqrt(-jnp.expm1(2.0 * log_a)) * (inp * uc.astype(jnp.float32))
    _, hseq = lax.associative_scan(linear_combine, (a, drive), axis=1)
    return hseq.astype(u.dtype)


def compress(t, pos, w1, w2):
    b, s, hk, dk = t.shape
    n_chunk = s // CMP_STRIDE
    ratio = CMP_BLOCK // CMP_STRIDE
    n_cmp = n_chunk - ratio + 1
    chunks = t.reshape(b, n_chunk, CMP_STRIDE, hk, dk)
    blocks = jnp.concatenate([chunks[:, r:r + n_cmp] for r in range(ratio)], axis=2)
    blocks = blocks + pos[:, None, :]
    flat = blocks.transpose(0, 1, 3, 2, 4).reshape(b, n_cmp, hk, CMP_BLOCK * dk)
    return jax.nn.silu(flat @ w1) @ w2


def nsa_branch(q, kc, vc, ks, vs, kw, vw, gates):
    b, s, h, dk = q.shape
    scale = dk ** -0.5
    n_cmp = kc.shape[1]
    n_sel = s // SEL_BLOCK
    n_top = min(SEL_TOPK, n_sel)
    nb = s // NSA_Q_BLOCK
    cmp_start = jnp.arange(n_cmp) * CMP_STRIDE
    cmp_end = cmp_start + CMP_BLOCK - 1
    sel_ids = jnp.arange(n_sel)
    sel_start = sel_ids * SEL_BLOCK
    overlap = ((cmp_start[:, None] < sel_start[None, :] + SEL_BLOCK)
               & (cmp_start[:, None] + CMP_BLOCK > sel_start[None, :])).astype(jnp.float32)
    ks_blk = ks.reshape(b, n_sel, SEL_BLOCK, NSA_KV_HEADS, dk).transpose(0, 3, 1, 2, 4)
    vs_blk = vs.reshape(b, n_sel, SEL_BLOCK, NSA_KV_HEADS, dk).transpose(0, 3, 1, 2, 4)
    bi = jnp.arange(b)[:, None, None, None]
    gi = jnp.arange(NSA_KV_HEADS)[None, None, :, None]
    pad = ((0, 0), (WINDOW, 0), (0, 0), (0, 0))
    kw_pad = jnp.pad(kw, pad)
    vw_pad = jnp.pad(vw, pad)
    win_off = jnp.arange(WINDOW + NSA_Q_BLOCK) - WINDOW
    q_blocks = q.reshape(b, nb, NSA_Q_BLOCK, NSA_KV_HEADS, NSA_GROUP, dk).swapaxes(0, 1)
    g_blocks = gates.reshape(b, nb, NSA_Q_BLOCK, NSA_KV_HEADS, NSA_GROUP, 3).swapaxes(0, 1)
    t0s = jnp.arange(nb, dtype=jnp.int32) * NSA_Q_BLOCK
    flat_sel = n_top * SEL_BLOCK

    def block(args):
        qb, gb, t0 = args
        tq = t0 + jnp.arange(NSA_Q_BLOCK)
        s_c = jnp.einsum('btgqd,bcgd->btgqc', qb, kc) * scale
        m_c = (cmp_end[None, :] <= tq[:, None])[None, :, None, None, :]
        p_c = masked_softmax(s_c, m_c)
        o_c = jnp.einsum('btgqc,bcgd->btgqd', p_c.astype(vc.dtype), vc)
        imp = jnp.einsum('btgqc,cj->btgj', p_c, overlap)
        cur = (tq // SEL_BLOCK)[:, None]
        valid = sel_ids[None, :] <= cur
        forced = (sel_ids[None, :] == 0) | (sel_ids[None, :] == cur) | (sel_ids[None, :] == cur - 1)
        score = jnp.where(forced[None, :, None, :], jnp.inf,
                          jnp.where(valid[None, :, None, :], imp, -jnp.inf))
        _, idx = lax.top_k(score, n_top)
        k_sel = ks_blk[bi, gi, idx]
        v_sel = vs_blk[bi, gi, idx]
        s_s = jnp.einsum('btgqd,btgnkd->btgqnk', qb, k_sel) * scale
        kpos = idx[..., None] * SEL_BLOCK + jnp.arange(SEL_BLOCK)
        m_s = kpos <= tq[None, :, None, None, None]
        p_s = masked_softmax(s_s.reshape(b, NSA_Q_BLOCK, NSA_KV_HEADS, NSA_GROUP, flat_sel),
                             m_s.reshape(b, NSA_Q_BLOCK, NSA_KV_HEADS, 1, flat_sel))
        o_s = jnp.einsum('btgqm,btgmd->btgqd', p_s.astype(vs.dtype),
                         v_sel.reshape(b, NSA_Q_BLOCK, NSA_KV_HEADS, flat_sel, dk))
        k_w = lax.dynamic_slice_in_dim(kw_pad, t0, WINDOW + NSA_Q_BLOCK, axis=1)
        v_w = lax.dynamic_slice_in_dim(vw_pad, t0, WINDOW + NSA_Q_BLOCK, axis=1)
        wpos = t0 + win_off
        m_w = ((wpos[None, :] >= 0) & (wpos[None, :] <= tq[:, None])
               & (wpos[None, :] > tq[:, None] - WINDOW))[None, :, None, None, :]
        s_w = jnp.einsum('btgqd,bkgd->btgqk', qb, k_w) * scale
        p_w = masked_softmax(s_w, m_w)
        o_w = jnp.einsum('btgqk,bkgd->btgqd', p_w.astype(vw.dtype), v_w)
        return gb[..., 0:1] * o_c + gb[..., 1:2] * o_s + gb[..., 2:3] * o_w

    out = lax.map(block, (q_blocks, g_blocks, t0s))
    return out.swapaxes(0, 1).reshape(b, s, h * dk)


def fox_branch(q, k, v, log_f):
    b, s, h, dk = q.shape
    scale = dk ** -0.5
    nb = s // FOX_Q_BLOCK
    c = jnp.cumsum(log_f, axis=1)
    ck = c.transpose(0, 2, 1)
    kpos = jnp.arange(s)
    q_blocks = q.reshape(b, nb, FOX_Q_BLOCK, h, dk).swapaxes(0, 1)
    c_blocks = c.reshape(b, nb, FOX_Q_BLOCK, h).swapaxes(0, 1)
    t0s = jnp.arange(nb, dtype=jnp.int32) * FOX_Q_BLOCK

    def block(args):
        qb, cb, t0 = args
        tq = t0 + jnp.arange(FOX_Q_BLOCK)
        logits = jnp.einsum('bqhd,bkhd->bhqk', qb, k).astype(jnp.float32) * scale
        logits = logits + (cb.transpose(0, 2, 1)[..., None] - ck[:, :, None, :])
        p_att = masked_softmax(logits, kpos[None, :] <= tq[:, None])
        return jnp.einsum('bhqk,bkhd->bqhd', p_att.astype(v.dtype), v)

    out = lax.map(block, (q_blocks, c_blocks, t0s))
    return out.swapaxes(0, 1).reshape(b, s, h * dk)


def setup_inputs(seed: int = 0) -> dict:
    key = jax.random.key(seed)
    ks = jax.random.split(key, 26)
    f32 = jnp.float32

    def nrm(k, shape, fan_in):
        return jax.random.normal(k, shape, f32) * (fan_in ** -0.5)

    def gain(k, shape):
        return 1.0 + 0.05 * jax.random.normal(k, shape, f32)

    def small(k, shape):
        return 0.02 * jax.random.normal(k, shape, f32)

    x = jax.random.normal(ks[0], (BATCH, SEQ, D_MODEL), f32)
    p = jax.random.normal(ks[1], (DEPTH, BATCH, SEQ, PLE_DIM), f32)
    ln_gain = gain(ks[2], (DEPTH, D_MODEL))
    w_in = nrm(ks[3], (DEPTH, D_MODEL, N_IN), D_MODEL)
    f_off = column_offsets()['fox_f'][0]
    b_in = small(ks[4], (DEPTH, N_IN)).at[:, f_off:f_off + FOX_HEADS].add(FORGET_BIAS)
    conv_w = nrm(ks[5], (DEPTH, CONV_WIDTH, BRANCH_WIDTH), CONV_WIDTH)
    conv_b = small(ks[6], (DEPTH, BRANCH_WIDTH))
    lru_wa = nrm(ks[7], (DEPTH, LRU_BLOCKS, LRU_BLOCK, LRU_BLOCK), LRU_BLOCK)
    lru_ba = small(ks[8], (DEPTH, BRANCH_WIDTH))
    lru_wx = nrm(ks[9], (DEPTH, LRU_BLOCKS, LRU_BLOCK, LRU_BLOCK), LRU_BLOCK)
    lru_bx = small(ks[10], (DEPTH, BRANCH_WIDTH))
    a0 = jax.random.uniform(ks[11], (DEPTH, BRANCH_WIDTH), f32, minval=0.9, maxval=0.999)
    root = a0 ** (1.0 / LRU_C)
    lru_lambda = jnp.log(root) - jnp.log1p(-root)
    cmp_w1 = nrm(ks[12], (DEPTH, 2, CMP_BLOCK * HEAD_DIM, CMP_HIDDEN), CMP_BLOCK * HEAD_DIM)
    cmp_w2 = nrm(ks[13], (DEPTH, 2, CMP_HIDDEN, HEAD_DIM), CMP_HIDDEN)
    cmp_pos = small(ks[14], (DEPTH, 2, CMP_BLOCK, HEAD_DIM))
    nsa_q_gain = gain(ks[15], (DEPTH, HEAD_DIM))
    nsa_k_gain = gain(ks[16], (DEPTH, HEAD_DIM))
    fox_q_gain = gain(ks[17], (DEPTH, HEAD_DIM))
    fox_k_gain = gain(ks[18], (DEPTH, HEAD_DIM))
    w_branch = nrm(ks[19], (DEPTH, N_BRANCH, BRANCH_WIDTH, D_MODEL), BRANCH_WIDTH)
    w_out = nrm(ks[20], (DEPTH, D_MODEL, D_MODEL), D_MODEL)
    w_ple = nrm(ks[21], (DEPTH, PLE_DIM, D_MODEL), PLE_DIM)
    ple_gain = gain(ks[22], (DEPTH, D_MODEL))
    w_ple_gate = nrm(ks[23], (DEPTH, D_MODEL, D_MODEL), D_MODEL)
    return {'x': x, 'p': p, 'ln_gain': ln_gain, 'w_in': w_in, 'b_in': b_in,
            'conv_w': conv_w, 'conv_b': conv_b, 'lru_wa': lru_wa, 'lru_ba': lru_ba,
            'lru_wx': lru_wx, 'lru_bx': lru_bx, 'lru_lambda': lru_lambda,
            'cmp_w1': cmp_w1, 'cmp_w2': cmp_w2, 'cmp_pos': cmp_pos,
            'nsa_q_gain': nsa_q_gain, 'nsa_k_gain': nsa_k_gain,
            'fox_q_gain': fox_q_gain, 'fox_k_gain': fox_k_gain,
            'w_branch': w_branch, 'w_out': w_out, 'w_ple': w_ple,
            'ple_gain': ple_gain, 'w_ple_gate': w_ple_gate}


def reference(x, p, ln_gain, w_in, b_in, conv_w, conv_b, lru_wa, lru_ba, lru_wx, lru_bx,
              lru_lambda, cmp_w1, cmp_w2, cmp_pos, nsa_q_gain, nsa_k_gain, fox_q_gain,
              fox_k_gain, w_branch, w_out, w_ple, ple_gain, w_ple_gate):
    b, s, _ = x.shape
    h = x
    for l in range(DEPTH):
        hn = rms_norm(h, ln_gain[l])
        z = split_columns(hn @ w_in[l] + b_in[l])
        y_a = rglru_branch(z['lru_x'], conv_w[l], conv_b[l], lru_wa[l], lru_ba[l],
                           lru_wx[l], lru_bx[l], lru_lambda[l]) * jax.nn.silu(z['lru_gate'])
        q_b = rms_norm(split_heads(z['nsa_q'], NSA_HEADS), nsa_q_gain[l])
        k_c = rms_norm(compress(split_heads(z['nsa_k_cmp'], NSA_KV_HEADS), cmp_pos[l, 0],
                                cmp_w1[l, 0], cmp_w2[l, 0]), nsa_k_gain[l])
        v_c = compress(split_heads(z['nsa_v_cmp'], NSA_KV_HEADS), cmp_pos[l, 1],
                       cmp_w1[l, 1], cmp_w2[l, 1])
        k_s = rms_norm(split_heads(z['nsa_k_slc'], NSA_KV_HEADS), nsa_k_gain[l])
        v_s = split_heads(z['nsa_v_slc'], NSA_KV_HEADS)
        k_w = rms_norm(split_heads(z['nsa_k_win'], NSA_KV_HEADS), nsa_k_gain[l])
        v_w = split_heads(z['nsa_v_win'], NSA_KV_HEADS)
        g_b = jax.nn.sigmoid(z['nsa_bgate']).reshape(b, s, NSA_HEADS, 3)
        y_b = nsa_branch(q_b, k_c, v_c, k_s, v_s, k_w, v_w, g_b) * jax.nn.silu(z['nsa_gate'])
        log_f = jax.nn.log_sigmoid(z['fox_f'].astype(jnp.float32))
        y_c = fox_branch(rms_norm(split_heads(z['fox_q'], FOX_HEADS), fox_q_gain[l]),
                         rms_norm(split_heads(z['fox_k'], FOX_HEADS), fox_k_gain[l]),
                         split_heads(z['fox_v'], FOX_HEADS), log_f) * jax.nn.silu(z['fox_gate'])
        mg = jax.nn.sigmoid(z['merge']).reshape(b, s, N_BRANCH, D_MODEL)
        merged = (mg[:, :, 0] * (y_a @ w_branch[l, 0])
                  + mg[:, :, 1] * (y_b @ w_branch[l, 1])
                  + mg[:, :, 2] * (y_c @ w_branch[l, 2]))
        h = h + merged @ w_out[l]
        e = rms_norm(p[l] @ w_ple[l], ple_gain[l])
        h = h + jax.nn.sigmoid(h @ w_ple_gate[l]) * e
    return h
```

```python
import functools

import jax
import jax.numpy as jnp
from jax import lax
from jax.experimental import pallas as pl
from jax.experimental.pallas import tpu as pltpu

F32 = jnp.float32
BF16 = jnp.bfloat16

HEAD_DIM = 128
NORM_EPS = 1e-6
LRU_BLOCK = 64
CONV_WIDTH = 4
LRU_C = 8.0
NSA_KV_HEADS = 2
NSA_GROUP = 4
CMP_BLOCK = 32
CMP_STRIDE = 16
SEL_BLOCK = 64
SEL_TOPK = 16
WINDOW = 512
N_BRANCH = 3

V7X_VMEM_BYTES = 64 * 1024 * 1024
VMEM_LIMIT = V7X_VMEM_BYTES - 8 * 1024 * 1024
LANES = 128

MASKED = -1e30
FORCED_SCORE = 1e30
INVALID_SCORE = -1.0


def _params(*semantics):
    return pltpu.CompilerParams(dimension_semantics=semantics, vmem_limit_bytes=VMEM_LIMIT)


def _dot(a, b):
    return jnp.dot(a, b, preferred_element_type=F32)


def _dot_t(a, b):
    return lax.dot_general(a, b, (((1,), (1,)), ((), ())), preferred_element_type=F32)


def _split3(x):
    hi = x.astype(BF16)
    r1 = x - hi.astype(F32)
    mid = r1.astype(BF16)
    lo = (r1 - mid.astype(F32)).astype(BF16)
    return hi, mid, lo


def _sigmoid(x):
    return 1.0 / (1.0 + jnp.exp(-x))


def _softplus(x):
    return jnp.maximum(x, 0.0) + jnp.log(1.0 + jnp.exp(-jnp.abs(x)))


def _rms(x, gain):
    y = x * lax.rsqrt(jnp.mean(x * x, axis=-1, keepdims=True) + NORM_EPS)
    return y * gain


def _prenorm_kernel(h_ref, g_ref, o_ref):
    o_ref[...] = _rms(h_ref[...], g_ref[...]).astype(BF16)


def _prenorm(h, gain, tm):
    t, d = h.shape
    return pl.pallas_call(
        _prenorm_kernel,
        out_shape=jax.ShapeDtypeStruct((t, d), BF16),
        grid=(t // tm,),
        in_specs=[pl.BlockSpec((tm, d), lambda i: (i, 0)), pl.BlockSpec((1, d), lambda i: (0, 0))],
        out_specs=pl.BlockSpec((tm, d), lambda i: (i, 0)),
        compiler_params=_params("parallel"),
        name="prenorm",
    )(h, gain)


def _matmul_kernel(*refs, epilogue):
    if epilogue in ("headnorm", "residual"):
        x_ref, w_ref, b_ref, e_ref, o_ref = refs
    else:
        x_ref, w_ref, b_ref, o_ref = refs
    acc = _dot(x_ref[...], w_ref[...]) + b_ref[...]
    if epilogue == "silu":
        acc = acc * _sigmoid(acc)
    elif epilogue == "sigmoid":
        acc = _sigmoid(acc)
    elif epilogue == "residual":
        acc = acc + e_ref[...]
    if epilogue == "headnorm":
        for c in range(acc.shape[1] // HEAD_DIM):
            cols = slice(c * HEAD_DIM, (c + 1) * HEAD_DIM)
            o_ref[:, cols] = _rms(acc[:, cols], e_ref[:, cols]).astype(o_ref.dtype)
    else:
        o_ref[...] = acc.astype(o_ref.dtype)


def _matmul(x, w, b, *, epilogue, out_dtype, tm, tn, extra=None, name):
    m, k = x.shape
    n = w.shape[1]
    in_specs = [pl.BlockSpec((tm, k), lambda i, j: (i, 0)),
                pl.BlockSpec((k, tn), lambda i, j: (0, j)),
                pl.BlockSpec((1, tn), lambda i, j: (0, j))]
    args = [x, w, b]
    if epilogue == "headnorm":
        in_specs.append(pl.BlockSpec((1, tn), lambda i, j: (0, j)))
        args.append(extra)
    elif epilogue == "residual":
        in_specs.append(pl.BlockSpec((tm, tn), lambda i, j: (i, j)))
        args.append(extra)
    return pl.pallas_call(
        functools.partial(_matmul_kernel, epilogue=epilogue),
        out_shape=jax.ShapeDtypeStruct((m, n), out_dtype),
        grid=(m // tm, n // tn),
        in_specs=in_specs,
        out_specs=pl.BlockSpec((tm, tn), lambda i, j: (i, j)),
        compiler_params=_params("parallel", "arbitrary"),
        name=name,
    )(*args)


LRU_PACK = 256


def _rglru_kernel(u_ref, g_ref, cw_ref, cb_ref, wa_ref, ba_ref, wx_ref, bx_ref, lam_ref, o_ref,
                  ubuf, abuf, bbuf, hbuf, hcar):
    ts, w = u_ref.shape
    s_idx = pl.program_id(1)

    @pl.when(s_idx == 0)
    def _():
        ubuf[0:8, :] = jnp.zeros((8, w), F32)
        hcar[...] = jnp.zeros((1, w), F32)

    ubuf[8:, :] = u_ref[...]
    uc = cb_ref[...] + cw_ref[CONV_WIDTH - 1:CONV_WIDTH, :] * ubuf[8:, :]
    for d in range(1, CONV_WIDTH):
        uc = uc + cw_ref[CONV_WIDTH - 1 - d:CONV_WIDTH - d, :] * ubuf[pl.ds(8 - d, ts), :]
    ubuf[0:8, :] = ubuf[ts:ts + 8, :]

    ucb = uc.astype(BF16)
    neg_c_softplus = -LRU_C * _softplus(-lam_ref[...])
    for c in range(w // LRU_PACK):
        cols = slice(c * LRU_PACK, (c + 1) * LRU_PACK)
        rec = _sigmoid(_dot(ucb[:, cols], wa_ref[c]) + ba_ref[:, cols])
        inp = _sigmoid(_dot(ucb[:, cols], wx_ref[c]) + bx_ref[:, cols])
        log_a = rec * neg_c_softplus[:, cols]
        a = jnp.exp(log_a)
        two_log_a = 2.0 * log_a
        one_minus_a2 = jnp.where(two_log_a > -0.5,
                                 -two_log_a * (1.0 + two_log_a * (0.5 + two_log_a * (1.0 / 6.0 + two_log_a * (
                                     1.0 / 24.0 + two_log_a * (1.0 / 120.0 + two_log_a * (
                                         1.0 / 720.0 + two_log_a * (1.0 / 5040.0 + two_log_a / 40320.0))))))),
                                 1.0 - a * a)
        abuf[:, cols] = a
        bbuf[:, cols] = jnp.sqrt(one_minus_a2) * (inp * uc[:, cols])

    def step(t, h):
        h = abuf[pl.ds(t, 1), :] * h + bbuf[pl.ds(t, 1), :]
        hbuf[pl.ds(t, 1), :] = h
        return h

    hcar[...] = lax.fori_loop(0, ts, step, hcar[...], unroll=8)
    o_ref[...] = (hbuf[...] * g_ref[...]).astype(o_ref.dtype)


def _rglru(za, zg, conv_w, conv_b, wa_bd, ba, wx_bd, bx, lam, *, batch, seq, ts):
    w = conv_w.shape[1]
    ns = seq // ts
    row = lambda b, s: (b * ns + s, 0)
    const = lambda b, s: (0, 0)
    return pl.pallas_call(
        _rglru_kernel,
        out_shape=jax.ShapeDtypeStruct((batch * seq, w), BF16),
        grid=(batch, ns),
        in_specs=[pl.BlockSpec((ts, w), row), pl.BlockSpec((ts, w), row),
                  pl.BlockSpec((CONV_WIDTH, w), const), pl.BlockSpec((1, w), const),
                  pl.BlockSpec(wa_bd.shape, lambda b, s: (0, 0, 0)), pl.BlockSpec((1, w), const),
                  pl.BlockSpec(wx_bd.shape, lambda b, s: (0, 0, 0)), pl.BlockSpec((1, w), const),
                  pl.BlockSpec((1, w), const)],
        out_specs=pl.BlockSpec((ts, w), row),
        scratch_shapes=[pltpu.VMEM((ts + 8, w), F32), pltpu.VMEM((ts, w), F32), pltpu.VMEM((ts, w), F32),
                        pltpu.VMEM((ts, w), F32), pltpu.VMEM((1, w), F32)],
        compiler_params=_params("parallel", "arbitrary"),
        name="rglru",
    )(za, zg, conv_w, conv_b, wa_bd, ba, wx_bd, bx, lam)


def _compress_kernel(x_ref, pos_ref, w1_ref, w2_ref, gain_ref, o_ref, xpad):
    seq = x_ref.shape[0]
    n_out = seq // CMP_STRIDE
    j = pl.program_id(1)
    xpad[0:seq, :] = x_ref[...]
    xpad[seq:, :] = jnp.zeros((CMP_BLOCK, HEAD_DIM), F32)
    hidden = jnp.zeros((n_out, w1_ref.shape[2]), F32)
    for r in range(CMP_BLOCK):
        rows = xpad[pl.ds(r, n_out, stride=CMP_STRIDE), :] + pos_ref[0, r:r + 1, :]
        hidden = hidden + _dot(rows.astype(BF16), w1_ref[0, r * HEAD_DIM:(r + 1) * HEAD_DIM, :])
    hidden = hidden * _sigmoid(hidden)
    out = _dot(hidden.astype(BF16), w2_ref[0])
    out = jnp.where(j < NSA_KV_HEADS, _rms(out, gain_ref[...]), out)
    row = lax.broadcasted_iota(jnp.int32, out.shape, 0)
    o_ref[0, 0] = jnp.where(row < n_out - 1, out, 0.0).astype(o_ref.dtype)


def _compress(zc, cmp_pos, cmp_w1, cmp_w2, k_gain, *, batch, seq):
    n_out = seq // CMP_STRIDE
    hid = cmp_w1.shape[2]
    return pl.pallas_call(
        _compress_kernel,
        out_shape=jax.ShapeDtypeStruct((batch, 2 * NSA_KV_HEADS, n_out, HEAD_DIM), BF16),
        grid=(batch, 2 * NSA_KV_HEADS),
        in_specs=[pl.BlockSpec((seq, HEAD_DIM), lambda b, j: (b, j)),
                  pl.BlockSpec((1, CMP_BLOCK, HEAD_DIM), lambda b, j: (j // NSA_KV_HEADS, 0, 0)),
                  pl.BlockSpec((1, CMP_BLOCK * HEAD_DIM, hid), lambda b, j: (j // NSA_KV_HEADS, 0, 0)),
                  pl.BlockSpec((1, hid, HEAD_DIM), lambda b, j: (j // NSA_KV_HEADS, 0, 0)),
                  pl.BlockSpec((1, HEAD_DIM), lambda b, j: (0, 0))],
        out_specs=pl.BlockSpec((1, 1, n_out, HEAD_DIM), lambda b, j: (b, j, 0, 0)),
        scratch_shapes=[pltpu.VMEM((seq + CMP_BLOCK, HEAD_DIM), F32)],
        compiler_params=_params("parallel", "arbitrary"),
        name="compress",
    )(zc, cmp_pos, cmp_w1, cmp_w2, k_gain)


def _forget_prefix_kernel(s_ref, o_ref, *, heads):
    seq = s_ref.shape[0]
    f = s_ref[...].T[0:heads, :]
    log_f = jnp.minimum(f, 0.0) - jnp.log(1.0 + jnp.exp(-jnp.abs(f)))
    r = lax.broadcasted_iota(jnp.int32, (LANES, LANES), 0)
    c = lax.broadcasted_iota(jnp.int32, (LANES, LANES), 1)
    upper = jnp.where(r <= c, 1.0, 0.0).astype(BF16)
    carry = jnp.zeros((heads, 1), F32)
    for k in range(seq // LANES):
        hi, mid, lo = _split3(log_f[:, k * LANES:(k + 1) * LANES])
        chunk = (_dot(lo, upper) + _dot(mid, upper)) + _dot(hi, upper) + carry
        o_ref[0, :, k * LANES:(k + 1) * LANES] = chunk
        carry = chunk[:, LANES - 1:LANES]


def _forget_prefix(zc, *, batch, seq, heads, col_block):
    return pl.pallas_call(
        functools.partial(_forget_prefix_kernel, heads=heads),
        out_shape=jax.ShapeDtypeStruct((batch, heads, seq), F32),
        grid=(batch,),
        in_specs=[pl.BlockSpec((seq, LANES), lambda b: (b, col_block))],
        out_specs=pl.BlockSpec((1, heads, seq), lambda b: (b, 0, 0)),
        compiler_params=_params("parallel"),
        name="forget_prefix",
    )(zc)


NSA_TQ = 128
NSA_TK = 512


def _masked_softmax(s, allowed):
    m = jnp.max(jnp.where(allowed, s, MASKED), axis=-1, keepdims=True)
    e = jnp.where(allowed, jnp.exp(s - m), 0.0)
    return e / jnp.maximum(jnp.sum(e, axis=-1, keepdims=True), 1e-30)


def _nsa_kernel(q_ref, kvc_ref, ks0_ref, ks1_ref, kw0_ref, kw1_ref, vs0_ref, vs1_ref, vw0_ref, vw1_ref,
                small_ref, gate_ref, o_ref, *, n_top, bgate_lane):
    tq = NSA_TQ
    rows = NSA_GROUP * tq
    seq = ks0_ref.shape[0]
    n_cmp_pad = kvc_ref.shape[2]
    n_sel = seq // SEL_BLOCK
    scale = HEAD_DIM ** -0.5
    t0 = pl.multiple_of(pl.program_id(1) * tq, tq)
    branch_gate = _sigmoid(small_ref[...])

    tok3 = t0 + lax.broadcasted_iota(jnp.int32, (NSA_GROUP, tq, 1), 1)
    tok_rows = tok3.reshape(rows, 1)

    for g, (ks_ref, kw_ref, vs_ref, vw_ref) in enumerate(
            ((ks0_ref, kw0_ref, vs0_ref, vw0_ref), (ks1_ref, kw1_ref, vs1_ref, vw1_ref))):
        q4 = jnp.concatenate(
            [q_ref[:, (g * NSA_GROUP + h) * HEAD_DIM:(g * NSA_GROUP + h + 1) * HEAD_DIM] for h in range(NSA_GROUP)],
            axis=0)

        kc = kvc_ref[0, g]
        vc = kvc_ref[0, NSA_KV_HEADS + g]
        cmp_end = lax.broadcasted_iota(jnp.int32, (1, n_cmp_pad), 1) * CMP_STRIDE + (CMP_BLOCK - 1)
        p_c = _masked_softmax(_dot_t(q4, kc) * scale, cmp_end <= tok_rows)
        o_c = _dot(p_c.astype(BF16), vc)

        p_sum = p_c[0:tq]
        for h in range(1, NSA_GROUP):
            p_sum = p_sum + p_c[h * tq:(h + 1) * tq]
        jj = lax.broadcasted_iota(jnp.int32, (n_sel, n_cmp_pad), 0)
        cc = lax.broadcasted_iota(jnp.int32, (n_sel, n_cmp_pad), 1)
        overlap_t = jnp.where((cc * CMP_STRIDE < (jj + 1) * SEL_BLOCK) & (cc * CMP_STRIDE + CMP_BLOCK > jj * SEL_BLOCK)
                              & (cc < n_cmp_pad - 1), 1.0, 0.0).astype(BF16)
        hi, mid, lo = _split3(p_sum)
        imp_t = (_dot_t(overlap_t, lo) + _dot_t(overlap_t, mid)) + _dot_t(overlap_t, hi)
        blk = lax.broadcasted_iota(jnp.int32, (n_sel, tq), 0)
        cur = (t0 + lax.broadcasted_iota(jnp.int32, (n_sel, tq), 1)) // SEL_BLOCK
        forced = (blk == 0) | (blk == cur) | (blk == cur - 1)
        score = jnp.where(forced, FORCED_SCORE, jnp.where(blk <= cur, imp_t, INVALID_SCORE))
        rank = jnp.zeros((n_sel, tq), F32)
        for i in range(n_sel):
            s_i = score[i:i + 1, :]
            ahead = (s_i > score) | ((s_i == score) & (blk > i))
            rank = rank + jnp.where(ahead, 1.0, 0.0)
        sel_tq = jnp.where(rank < n_top, 1.0, 0.0).T.astype(BF16)

        def sel_tile(kt, carry, causal):
            m, l, acc = carry
            k0 = pl.multiple_of(kt * NSA_TK, NSA_TK)
            s = _dot_t(q4, ks_ref[pl.ds(k0, NSA_TK), :]) * scale
            blk_of_key = (k0 + lax.broadcasted_iota(jnp.int32, (n_sel, NSA_TK), 1)) // SEL_BLOCK
            expand = jnp.where(blk_of_key == lax.broadcasted_iota(jnp.int32, (n_sel, NSA_TK), 0), 1.0, 0.0)
            allowed = (_dot(sel_tq, expand.astype(BF16)) > 0.5)[None]
            if causal:
                kpos = k0 + lax.broadcasted_iota(jnp.int32, (1, 1, NSA_TK), 2)
                allowed = allowed & (kpos <= tok3)
            allowed = jnp.broadcast_to(allowed, (NSA_GROUP, tq, NSA_TK)).reshape(rows, NSA_TK)
            m_new = jnp.maximum(m, jnp.max(jnp.where(allowed, s, MASKED), axis=-1, keepdims=True))
            alpha = jnp.exp(m - m_new)
            e = jnp.where(allowed, jnp.exp(s - m_new), 0.0)
            l = alpha * l + jnp.sum(e, axis=-1, keepdims=True)
            acc = alpha * acc + _dot(e.astype(BF16), vs_ref[pl.ds(k0, NSA_TK), :])
            return m_new, l, acc

        kd = t0 // NSA_TK
        init = (jnp.full((rows, 1), MASKED, F32), jnp.zeros((rows, 1), F32), jnp.zeros((rows, HEAD_DIM), F32))
        carry = lax.fori_loop(0, kd, functools.partial(sel_tile, causal=False), init)
        _, l, acc = sel_tile(kd, carry, True)
        o_s = acc / jnp.maximum(l, 1e-30)

        w0 = pl.multiple_of(jnp.maximum(t0 - WINDOW, 0), tq)
        wpos = w0 + lax.broadcasted_iota(jnp.int32, (1, WINDOW + tq), 1)
        p_w = _masked_softmax(_dot_t(q4, kw_ref[pl.ds(w0, WINDOW + tq), :]) * scale,
                              (wpos <= tok_rows) & (wpos > tok_rows - WINDOW))
        o_w = _dot(p_w.astype(BF16), vw_ref[pl.ds(w0, WINDOW + tq), :])

        for h in range(NSA_GROUP):
            head = g * NSA_GROUP + h
            r = slice(h * tq, (h + 1) * tq)
            lane = bgate_lane + head * N_BRANCH
            mix = (branch_gate[:, lane:lane + 1] * o_c[r] + branch_gate[:, lane + 1:lane + 2] * o_s[r]
                   + branch_gate[:, lane + 2:lane + 3] * o_w[r])
            cols = slice(head * HEAD_DIM, (head + 1) * HEAD_DIM)
            o_ref[:, cols] = (mix * gate_ref[:, cols]).astype(o_ref.dtype)


def _nsa(zq, zv, kvc, zc, zg, *, batch, seq, cols):
    nq = seq // NSA_TQ
    width = NSA_KV_HEADS * NSA_GROUP * HEAD_DIM
    n_top = min(SEL_TOPK, seq // SEL_BLOCK)
    row = lambda b, i: b * nq + i
    kv_spec = lambda c: pl.BlockSpec((seq, HEAD_DIM), lambda b, i: (b, c))
    return pl.pallas_call(
        functools.partial(_nsa_kernel, n_top=n_top, bgate_lane=cols["bgate_lane"]),
        out_shape=jax.ShapeDtypeStruct((batch * seq, width), BF16),
        grid=(batch, nq),
        in_specs=[pl.BlockSpec((NSA_TQ, width), lambda b, i: (row(b, i), cols["nsa_q"] * HEAD_DIM // width)),
                  pl.BlockSpec((1,) + kvc.shape[1:], lambda b, i: (b, 0, 0, 0)),
                  kv_spec(cols["k_slc"]), kv_spec(cols["k_slc"] + 1),
                  kv_spec(cols["k_win"]), kv_spec(cols["k_win"] + 1),
                  kv_spec(cols["v_slc"]), kv_spec(cols["v_slc"] + 1),
                  kv_spec(cols["v_win"]), kv_spec(cols["v_win"] + 1),
                  pl.BlockSpec((NSA_TQ, LANES), lambda b, i: (row(b, i), cols["small"])),
                  pl.BlockSpec((NSA_TQ, width), lambda b, i: (row(b, i), cols["nsa_gate"] * HEAD_DIM // width))],
        out_specs=pl.BlockSpec((NSA_TQ, width), lambda b, i: (row(b, i), 0)),
        compiler_params=_params("parallel", "arbitrary"),
        name="sparse_attention",
    )(zq, kvc, zq, zq, zq, zq, zv, zv, zv, zv, zc, zg)


FOX_T = 256


def _fox_kernel(q_ref, k_ref, v_ref, c_ref, gate_ref, o_ref):
    t = FOX_T
    scale = HEAD_DIM ** -0.5
    i = pl.program_id(2)
    q = q_ref[...]
    q0 = pl.multiple_of(i * t, t)
    c_tile_end = c_ref[0, :, pl.ds(q0, t)][:, t - 1:t]

    def tile(kt, carry, causal):
        m, l, acc = carry
        k0 = pl.multiple_of(kt * t, t)
        s = _dot_t(q, k_ref[pl.ds(k0, t), :]) * scale + (c_tile_end - c_ref[0, :, pl.ds(k0, t)])
        if causal:
            qpos = lax.broadcasted_iota(jnp.int32, (t, t), 0)
            kpos = lax.broadcasted_iota(jnp.int32, (t, t), 1)
            allowed = kpos <= qpos
            s = jnp.where(allowed, s, MASKED)
        m_new = jnp.maximum(m, jnp.max(s, axis=-1, keepdims=True))
        alpha = jnp.exp(m - m_new)
        e = jnp.exp(s - m_new)
        if causal:
            e = jnp.where(allowed, e, 0.0)
        l = alpha * l + jnp.sum(e, axis=-1, keepdims=True)
        acc = alpha * acc + _dot(e.astype(BF16), v_ref[pl.ds(k0, t), :])
        return m_new, l, acc

    init = (jnp.full((t, 1), MASKED, F32), jnp.zeros((t, 1), F32), jnp.zeros((t, HEAD_DIM), F32))
    carry = lax.fori_loop(0, i, functools.partial(tile, causal=False), init)
    _, l, acc = tile(i, carry, True)
    o_ref[...] = (acc / l * gate_ref[...]).astype(o_ref.dtype)


def _fox(zq, zv, cpre, zg, *, batch, seq, heads, cols):
    nq = seq // FOX_T
    row = lambda b, h, i: b * nq + i
    return pl.pallas_call(
        _fox_kernel,
        out_shape=jax.ShapeDtypeStruct((batch * seq, heads * HEAD_DIM), BF16),
        grid=(batch, heads, nq),
        in_specs=[pl.BlockSpec((FOX_T, HEAD_DIM), lambda b, h, i: (row(b, h, i), cols["fox_q"] + h)),
                  pl.BlockSpec((seq, HEAD_DIM), lambda b, h, i: (b, cols["fox_k"] + h)),
                  pl.BlockSpec((seq, HEAD_DIM), lambda b, h, i: (b, cols["fox_v"] + h)),
                  pl.BlockSpec((1, 1, seq), lambda b, h, i: (b * heads + h, 0, 0)),
                  pl.BlockSpec((FOX_T, HEAD_DIM), lambda b, h, i: (row(b, h, i), cols["fox_gate"] + h))],
        out_specs=pl.BlockSpec((FOX_T, HEAD_DIM), lambda b, h, i: (row(b, h, i), h)),
        compiler_params=_params("parallel", "parallel", "arbitrary"),
        name="forgetting_attention",
    )(zq, zq, zv, cpre.reshape(batch * heads, 1, seq), zg)


def _merge_kernel(ya_ref, yb_ref, yc_ref, w_ref, ga_ref, gb_ref, gc_ref, o_ref):
    merged = ga_ref[...] * _dot(ya_ref[...], w_ref[0])
    merged = merged + gb_ref[...] * _dot(yb_ref[...], w_ref[1])
    merged = merged + gc_ref[...] * _dot(yc_ref[...], w_ref[2])
    o_ref[...] = merged.astype(o_ref.dtype)


def _merge(ya, yb, yc, w_branch, zm, *, tm, tn):
    m, k = ya.shape
    n = w_branch.shape[2]
    y_spec = pl.BlockSpec((tm, k), lambda i, j: (i, 0))
    g_spec = lambda br: pl.BlockSpec((tm, tn), lambda i, j: (i, br * (n // tn) + j))
    return pl.pallas_call(
        _merge_kernel,
        out_shape=jax.ShapeDtypeStruct((m, n), BF16),
        grid=(m // tm, n // tn),
        in_specs=[y_spec, y_spec, y_spec, pl.BlockSpec((N_BRANCH, k, tn), lambda i, j: (0, 0, j)),
                  g_spec(0), g_spec(1), g_spec(2)],
        out_specs=pl.BlockSpec((tm, tn), lambda i, j: (i, j)),
        compiler_params=_params("parallel", "arbitrary"),
        name="merge",
    )(ya, yb, yc, w_branch, zm, zm, zm)


def _ple_kernel(h_ref, p_ref, wp_ref, pg_ref, wg_ref, ng_ref, h_out_ref, hn_out_ref):
    h = h_ref[...]
    e = _rms(_dot(p_ref[...].astype(BF16), wp_ref[...]), pg_ref[...])
    h = h + _sigmoid(_dot(h.astype(BF16), wg_ref[...])) * e
    h_out_ref[...] = h
    hn_out_ref[...] = _rms(h, ng_ref[...]).astype(BF16)


def _ple(h, p, w_ple, ple_gain, w_gate, next_gain, *, tm):
    t, d = h.shape
    pd = p.shape[1]
    const = lambda i: (0, 0)
    return pl.pallas_call(
        _ple_kernel,
        out_shape=(jax.ShapeDtypeStruct((t, d), F32), jax.ShapeDtypeStruct((t, d), BF16)),
        grid=(t // tm,),
        in_specs=[pl.BlockSpec((tm, d), lambda i: (i, 0)), pl.BlockSpec((tm, pd), lambda i: (i, 0)),
                  pl.BlockSpec((pd, d), const), pl.BlockSpec((1, d), const),
                  pl.BlockSpec((d, d), const), pl.BlockSpec((1, d), const)],
        out_specs=(pl.BlockSpec((tm, d), lambda i: (i, 0)), pl.BlockSpec((tm, d), lambda i: (i, 0))),
        compiler_params=_params("parallel"),
        name="ple_update",
    )(h, p, w_ple, ple_gain, w_gate, next_gain)


def _in_proj_layout(d_model, branch_width):
    kvw = NSA_KV_HEADS * HEAD_DIM
    heads = branch_width // HEAD_DIM
    splits = (("lru_x", branch_width), ("lru_gate", branch_width), ("nsa_q", branch_width),
              ("nsa_k_cmp", kvw), ("nsa_v_cmp", kvw), ("nsa_k_slc", kvw), ("nsa_v_slc", kvw),
              ("nsa_k_win", kvw), ("nsa_v_win", kvw), ("nsa_bgate", heads * N_BRANCH),
              ("nsa_gate", branch_width), ("fox_q", branch_width), ("fox_k", branch_width),
              ("fox_v", branch_width), ("fox_f", heads), ("fox_gate", branch_width),
              ("merge", N_BRANCH * d_model))
    offs, start = {}, 0
    for name, size in splits:
        offs[name] = (start, size)
        start += size
    return offs


_GROUPS = {
    "za": (("lru_x",), "none", F32),
    "zg": (("lru_gate", "nsa_gate", "fox_gate"), "silu", F32),
    "zq": (("nsa_q", "nsa_k_slc", "nsa_k_win", "fox_q", "fox_k"), "headnorm", BF16),
    "zv": (("nsa_v_slc", "nsa_v_win", "fox_v"), "none", BF16),
    "zc": (("nsa_k_cmp", "nsa_v_cmp", "fox_f", "nsa_bgate"), "none", F32),
    "zm": (("merge",), "sigmoid", F32),
}


def _gather_cols(w, offs, names, pad_to=None):
    parts = [lax.slice_in_dim(w, offs[n][0], offs[n][0] + offs[n][1], axis=w.ndim - 1) for n in names]
    out = jnp.concatenate(parts, axis=-1) if len(parts) > 1 else parts[0]
    if pad_to is not None and out.shape[-1] % pad_to:
        pad = pad_to - out.shape[-1] % pad_to
        out = jnp.pad(out, [(0, 0)] * (out.ndim - 1) + [(0, pad)])
    return out


def _block_diag(w, pack):
    depth, nb, k, _ = w.shape
    per = pack // k
    w = w.reshape(depth, nb // per, per, k, k)
    eye = jnp.eye(per, dtype=w.dtype)
    return jnp.einsum("dgpij,pq->dgpiqj", w, eye).reshape(depth, nb // per, pack, pack)


def _tile(n, target):
    if n <= target:
        return n
    best = LANES
    for t in range(LANES, target + 1, LANES):
        if n % t == 0:
            best = t
    return best


def kernel(x, p, ln_gain, w_in, b_in, conv_w, conv_b, lru_wa, lru_ba, lru_wx, lru_bx, lru_lambda, cmp_w1, cmp_w2,
           cmp_pos, nsa_q_gain, nsa_k_gain, fox_q_gain, fox_k_gain, w_branch, w_out, w_ple, ple_gain, w_ple_gate):
    batch, seq, d_model = x.shape
    depth = w_in.shape[0]
    bw = conv_w.shape[2]
    heads = bw // HEAD_DIM
    t = batch * seq
    assert seq % NSA_TK == 0 and seq % FOX_T == 0 and seq >= WINDOW + NSA_TQ
    offs = _in_proj_layout(d_model, bw)

    wg, bg = {}, {}
    for name, (members, _, _) in _GROUPS.items():
        wg[name] = _gather_cols(w_in, offs, members, pad_to=LANES).astype(BF16)
        bg[name] = _gather_cols(b_in, offs, members, pad_to=LANES)[:, None, :]
    rep = lambda g, n: jnp.tile(g, (1, n))
    zq_gain = jnp.concatenate([rep(nsa_q_gain, heads), rep(nsa_k_gain, 2 * NSA_KV_HEADS),
                               rep(fox_q_gain, heads), rep(fox_k_gain, heads)], axis=1)[:, None, :]
    hb = bw // HEAD_DIM
    cols = {"nsa_q": 0, "k_slc": hb, "k_win": hb + NSA_KV_HEADS, "fox_q": hb + 2 * NSA_KV_HEADS,
            "fox_k": 2 * hb + 2 * NSA_KV_HEADS,
            "v_slc": 0, "v_win": NSA_KV_HEADS, "fox_v": 2 * NSA_KV_HEADS,
            "small": 2 * NSA_KV_HEADS, "bgate_lane": heads,
            "nsa_gate": hb, "fox_gate": 2 * hb}
    wa_bd = _block_diag(lru_wa, LRU_PACK).astype(BF16)
    wx_bd = _block_diag(lru_wx, LRU_PACK).astype(BF16)
    w_branch_b = w_branch.astype(BF16)
    w_out_b = w_out.astype(BF16)
    w_ple_b = w_ple.astype(BF16)
    w_gate_b = w_ple_gate.astype(BF16)
    cmp_w1_b = cmp_w1.astype(BF16)
    cmp_w2_b = cmp_w2.astype(BF16)
    zero_bias = jnp.zeros((1, d_model), F32)

    tm = _tile(t, 1024)
    h = x.reshape(t, d_model)
    p2 = p.reshape(depth, t, p.shape[-1])
    hn = _prenorm(h, ln_gain[0][None], _tile(t, 512))
    for l in range(depth):
        z = {}
        for name, (_, epilogue, dtype) in _GROUPS.items():
            n = wg[name].shape[2]
            z[name] = _matmul(hn, wg[name][l], bg[name][l], epilogue=epilogue, out_dtype=dtype, tm=tm,
                              tn=_tile(n, 640), extra=zq_gain[l] if epilogue == "headnorm" else None,
                              name="in_proj_" + name)
        ya = _rglru(z["za"], z["zg"], conv_w[l], conv_b[l][None], wa_bd[l], lru_ba[l][None], wx_bd[l],
                    lru_bx[l][None], lru_lambda[l][None], batch=batch, seq=seq, ts=_tile(seq, 512))
        kvc = _compress(z["zc"], cmp_pos[l], cmp_w1_b[l], cmp_w2_b[l], nsa_k_gain[l][None], batch=batch, seq=seq)
        yb = _nsa(z["zq"], z["zv"], kvc, z["zc"], z["zg"], batch=batch, seq=seq, cols=cols)
        cpre = _forget_prefix(z["zc"], batch=batch, seq=seq, heads=heads, col_block=cols["small"])
        yc = _fox(z["zq"], z["zv"], cpre, z["zg"], batch=batch, seq=seq, heads=heads, cols=cols)
        merged = _merge(ya, yb, yc, w_branch_b[l], z["zm"], tm=tm, tn=_tile(d_model, 512))
        h = _matmul(merged, w_out_b[l], zero_bias, epilogue="residual", out_dtype=F32, tm=tm,
                    tn=_tile(d_model, 512), extra=h, name="out_proj")
        next_gain = ln_gain[(l + 1) % depth][None]
        h, hn = _ple(h, p2[l], w_ple_b[l], ple_gain[l][None], w_gate_b[l], next_gain, tm=_tile(t, 512))
    return h.reshape(batch, seq, d_model)
```

```python
import functools

import jax
import jax.numpy as jnp
from jax import lax
from jax.experimental import pallas as pl
from jax.experimental.pallas import tpu as pltpu

F32 = jnp.float32
BF16 = jnp.bfloat16

HEAD_DIM = 128
NORM_EPS = 1e-6
LRU_BLOCK = 64
CONV_WIDTH = 4
LRU_C = 8.0
NSA_KV_HEADS = 2
NSA_GROUP = 4
CMP_BLOCK = 32
CMP_STRIDE = 16
SEL_BLOCK = 64
SEL_TOPK = 16
WINDOW = 512
N_BRANCH = 3

V7X_VMEM_BYTES = 64 * 1024 * 1024
VMEM_LIMIT = V7X_VMEM_BYTES - 8 * 1024 * 1024
LANES = 128

MASKED = -1e30
FORCED_SCORE = 1e30
INVALID_SCORE = -1.0


def _params(*semantics):
    return pltpu.CompilerParams(dimension_semantics=semantics, vmem_limit_bytes=VMEM_LIMIT)


def _dot(a, b):
    return jnp.dot(a, b, preferred_element_type=F32)


def _dot_t(a, b):
    return lax.dot_general(a, b, (((1,), (1,)), ((), ())), preferred_element_type=F32)


def _split3(x):
    hi = x.astype(BF16)
    r1 = x - hi.astype(F32)
    mid = r1.astype(BF16)
    lo = (r1 - mid.astype(F32)).astype(BF16)
    return hi, mid, lo


def _sigmoid(x):
    return 1.0 / (1.0 + jnp.exp(-x))


def _softplus(x):
    return jnp.maximum(x, 0.0) + jnp.log(1.0 + jnp.exp(-jnp.abs(x)))


def _rms(x, gain):
    y = x * lax.rsqrt(jnp.mean(x * x, axis=-1, keepdims=True) + NORM_EPS)
    return y * gain


def _prenorm_kernel(h_ref, g_ref, o_ref):
    o_ref[...] = _rms(h_ref[...], g_ref[...]).astype(BF16)


def _prenorm(h, gain, tm):
    t, d = h.shape
    return pl.pallas_call(
        _prenorm_kernel,
        out_shape=jax.ShapeDtypeStruct((t, d), BF16),
        grid=(t // tm,),
        in_specs=[pl.BlockSpec((tm, d), lambda i: (i, 0)), pl.BlockSpec((1, d), lambda i: (0, 0))],
        out_specs=pl.BlockSpec((tm, d), lambda i: (i, 0)),
        compiler_params=_params("parallel"),
        name="prenorm",
    )(h, gain)


def _matmul_kernel(*refs, epilogue):
    if epilogue in ("headnorm", "residual"):
        x_ref, w_ref, b_ref, e_ref, o_ref = refs
    else:
        x_ref, w_ref, b_ref, o_ref = refs
    acc = _dot(x_ref[...], w_ref[...]) + b_ref[...]
    if epilogue == "silu":
        acc = acc * _sigmoid(acc)
    elif epilogue == "sigmoid":
        acc = _sigmoid(acc)
    elif epilogue == "residual":
        acc = acc + e_ref[...]
    if epilogue == "headnorm":
        for c in range(acc.shape[1] // HEAD_DIM):
            cols = slice(c * HEAD_DIM, (c + 1) * HEAD_DIM)
            o_ref[:, cols] = _rms(acc[:, cols], e_ref[:, cols]).astype(o_ref.dtype)
    else:
        o_ref[...] = acc.astype(o_ref.dtype)


def _matmul(x, w, b, *, epilogue, out_dtype, tm, tn, extra=None, name):
    m, k = x.shape
    n = w.shape[1]
    in_specs = [pl.BlockSpec((tm, k), lambda i, j: (i, 0)),
                pl.BlockSpec((k, tn), lambda i, j: (0, j)),
                pl.BlockSpec((1, tn), lambda i, j: (0, j))]
    args = [x, w, b]
    if epilogue == "headnorm":
        in_specs.append(pl.BlockSpec((1, tn), lambda i, j: (0, j)))
        args.append(extra)
    elif epilogue == "residual":
        in_specs.append(pl.BlockSpec((tm, tn), lambda i, j: (i, j)))
        args.append(extra)
    return pl.pallas_call(
        functools.partial(_matmul_kernel, epilogue=epilogue),
        out_shape=jax.ShapeDtypeStruct((m, n), out_dtype),
        grid=(m // tm, n // tn),
        in_specs=in_specs,
        out_specs=pl.BlockSpec((tm, tn), lambda i, j: (i, j)),
        compiler_params=_params("parallel", "arbitrary"),
        name=name,
    )(*args)


LRU_PACK = 256


def _rglru_kernel(u_ref, g_ref, cw_ref, cb_ref, wa_ref, ba_ref, wx_ref, bx_ref, lam_ref, o_ref,
                  ubuf, abuf, bbuf, hbuf, hcar):
    ts, w = u_ref.shape
    s_idx = pl.program_id(1)

    @pl.when(s_idx == 0)
    def _():
        ubuf[0:8, :] = jnp.zeros((8, w), F32)
        hcar[...] = jnp.zeros((1, w), F32)

    ubuf[8:, :] = u_ref[...]
    uc = cb_ref[...] + cw_ref[CONV_WIDTH - 1:CONV_WIDTH, :] * ubuf[8:, :]
    for d in range(1, CONV_WIDTH):
        uc = uc + cw_ref[CONV_WIDTH - 1 - d:CONV_WIDTH - d, :] * ubuf[pl.ds(8 - d, ts), :]
    ubuf[0:8, :] = ubuf[ts:ts + 8, :]

    ucb = uc.astype(BF16)
    neg_c_softplus = -LRU_C * _softplus(-lam_ref[...])
    for c in range(w // LRU_PACK):
        cols = slice(c * LRU_PACK, (c + 1) * LRU_PACK)
        rec = _sigmoid(_dot(ucb[:, cols], wa_ref[c]) + ba_ref[:, cols])
        inp = _sigmoid(_dot(ucb[:, cols], wx_ref[c]) + bx_ref[:, cols])
        log_a = rec * neg_c_softplus[:, cols]
        a = jnp.exp(log_a)
        two_log_a = 2.0 * log_a
        one_minus_a2 = jnp.where(two_log_a > -0.5,
                                 -two_log_a * (1.0 + two_log_a * (0.5 + two_log_a * (1.0 / 6.0 + two_log_a * (
                                     1.0 / 24.0 + two_log_a * (1.0 / 120.0 + two_log_a * (
                                         1.0 / 720.0 + two_log_a * (1.0 / 5040.0 + two_log_a / 40320.0))))))),
                                 1.0 - a * a)
        abuf[:, cols] = a
        bbuf[:, cols] = jnp.sqrt(one_minus_a2) * (inp * uc[:, cols])

    def step(t, h):
        h = abuf[pl.ds(t, 1), :] * h + bbuf[pl.ds(t, 1), :]
        hbuf[pl.ds(t, 1), :] = h
        return h

    hcar[...] = lax.fori_loop(0, ts, step, hcar[...], unroll=8)
    o_ref[...] = (hbuf[...] * g_ref[...]).astype(o_ref.dtype)


def _rglru(za, zg, conv_w, conv_b, wa_bd, ba, wx_bd, bx, lam, *, batch, seq, ts):
    w = conv_w.shape[1]
    ns = seq // ts
    row = lambda b, s: (b * ns + s, 0)
    const = lambda b, s: (0, 0)
    return pl.pallas_call(
        _rglru_kernel,
        out_shape=jax.ShapeDtypeStruct((batch * seq, w), BF16),
        grid=(batch, ns),
        in_specs=[pl.BlockSpec((ts, w), row), pl.BlockSpec((ts, w), row),
                  pl.BlockSpec((CONV_WIDTH, w), const), pl.BlockSpec((1, w), const),
                  pl.BlockSpec(wa_bd.shape, lambda b, s: (0, 0, 0)), pl.BlockSpec((1, w), const),
                  pl.BlockSpec(wx_bd.shape, lambda b, s: (0, 0, 0)), pl.BlockSpec((1, w), const),
                  pl.BlockSpec((1, w), const)],
        out_specs=pl.BlockSpec((ts, w), row),
        scratch_shapes=[pltpu.VMEM((ts + 8, w), F32), pltpu.VMEM((ts, w), F32), pltpu.VMEM((ts, w), F32),
                        pltpu.VMEM((ts, w), F32), pltpu.VMEM((1, w), F32)],
        compiler_params=_params("parallel", "arbitrary"),
        name="rglru",
    )(za, zg, conv_w, conv_b, wa_bd, ba, wx_bd, bx, lam)


def _compress_kernel(x_ref, pos_ref, w1_ref, w2_ref, gain_ref, o_ref, xpad):
    seq = x_ref.shape[0]
    n_out = seq // CMP_STRIDE
    j = pl.program_id(1)
    xpad[0:seq, :] = x_ref[...]
    xpad[seq:, :] = jnp.zeros((CMP_BLOCK, HEAD_DIM), F32)
    hidden = jnp.zeros((n_out, w1_ref.shape[2]), F32)
    for r in range(CMP_BLOCK):
        rows = xpad[pl.ds(r, n_out, stride=CMP_STRIDE), :] + pos_ref[0, r:r + 1, :]
        hidden = hidden + _dot(rows.astype(BF16), w1_ref[0, r * HEAD_DIM:(r + 1) * HEAD_DIM, :])
    hidden = hidden * _sigmoid(hidden)
    out = _dot(hidden.astype(BF16), w2_ref[0])
    out = jnp.where(j < NSA_KV_HEADS, _rms(out, gain_ref[...]), out)
    row = lax.broadcasted_iota(jnp.int32, out.shape, 0)
    o_ref[0, 0] = jnp.where(row < n_out - 1, out, 0.0).astype(o_ref.dtype)


def _compress(zc, cmp_pos, cmp_w1, cmp_w2, k_gain, *, batch, seq):
    n_out = seq // CMP_STRIDE
    hid = cmp_w1.shape[2]
    return pl.pallas_call(
        _compress_kernel,
        out_shape=jax.ShapeDtypeStruct((batch, 2 * NSA_KV_HEADS, n_out, HEAD_DIM), BF16),
        grid=(batch, 2 * NSA_KV_HEADS),
        in_specs=[pl.BlockSpec((seq, HEAD_DIM), lambda b, j: (b, j)),
                  pl.BlockSpec((1, CMP_BLOCK, HEAD_DIM), lambda b, j: (j // NSA_KV_HEADS, 0, 0)),
                  pl.BlockSpec((1, CMP_BLOCK * HEAD_DIM, hid), lambda b, j: (j // NSA_KV_HEADS, 0, 0)),
                  pl.BlockSpec((1, hid, HEAD_DIM), lambda b, j: (j // NSA_KV_HEADS, 0, 0)),
                  pl.BlockSpec((1, HEAD_DIM), lambda b, j: (0, 0))],
        out_specs=pl.BlockSpec((1, 1, n_out, HEAD_DIM), lambda b, j: (b, j, 0, 0)),
        scratch_shapes=[pltpu.VMEM((seq + CMP_BLOCK, HEAD_DIM), F32)],
        compiler_params=_params("parallel", "arbitrary"),
        name="compress",
    )(zc, cmp_pos, cmp_w1, cmp_w2, k_gain)


def _forget_prefix_kernel(s_ref, o_ref, *, heads):
    seq = s_ref.shape[0]
    f = s_ref[...].T[0:heads, :]
    log_f = jnp.minimum(f, 0.0) - jnp.log(1.0 + jnp.exp(-jnp.abs(f)))
    r = lax.broadcasted_iota(jnp.int32, (LANES, LANES), 0)
    c = lax.broadcasted_iota(jnp.int32, (LANES, LANES), 1)
    upper = jnp.where(r <= c, 1.0, 0.0).astype(BF16)
    carry = jnp.zeros((heads, 1), F32)
    for k in range(seq // LANES):
        hi, mid, lo = _split3(log_f[:, k * LANES:(k + 1) * LANES])
        chunk = (_dot(lo, upper) + _dot(mid, upper)) + _dot(hi, upper) + carry
        o_ref[0, :, k * LANES:(k + 1) * LANES] = chunk
        carry = chunk[:, LANES - 1:LANES]


def _forget_prefix(zc, *, batch, seq, heads, col_block):
    return pl.pallas_call(
        functools.partial(_forget_prefix_kernel, heads=heads),
        out_shape=jax.ShapeDtypeStruct((batch, heads, seq), F32),
        grid=(batch,),
        in_specs=[pl.BlockSpec((seq, LANES), lambda b: (b, col_block))],
        out_specs=pl.BlockSpec((1, heads, seq), lambda b: (b, 0, 0)),
        compiler_params=_params("parallel"),
        name="forget_prefix",
    )(zc)


NSA_TQ = 256
NSA_TK = 512
SEL_SHIFT = SEL_BLOCK.bit_length() - 1
SUBLANES = 8
LOG2E = 1.4426950408889634


def _biased_exp2(s, bias):
    rows, k = s.shape
    y = (s.reshape(NSA_GROUP, rows // NSA_GROUP, k) + bias[None]).reshape(rows, k)
    e = jnp.exp2(y - jnp.max(y, axis=-1, keepdims=True))
    return e, jnp.sum(e, axis=-1, keepdims=True)


def _top_rows_bias(score, n_top):
    n, t = score.shape
    n_groups = n // SUBLANES
    groups = [score[r * SUBLANES:(r + 1) * SUBLANES] for r in range(n_groups)]
    ranks = [jnp.zeros((SUBLANES, t), F32) for _ in range(n_groups)]
    local = lax.broadcasted_iota(jnp.int32, (SUBLANES, t), 0)
    for i in range(n):
        gi, li = divmod(i, SUBLANES)
        s_i = groups[gi][li:li + 1, :]
        for r in range(n_groups):
            if r > gi:
                ahead = s_i >= groups[r]
            elif r < gi:
                ahead = s_i > groups[r]
            else:
                ahead = (s_i > groups[r]) | ((s_i == groups[r]) & (local > li))
            ranks[r] = ranks[r] + jnp.where(ahead, 1.0, 0.0)
    return jnp.where(jnp.concatenate(ranks, axis=0) < n_top, 0.0, MASKED)


def _nsa_kernel(q_ref, kvc_ref, ks0_ref, ks1_ref, kw0_ref, kw1_ref, vs0_ref, vs1_ref, vw0_ref, vw1_ref,
                small_ref, gate_ref, o_ref, *, n_top, bgate_lane):
    tq = NSA_TQ
    rows = NSA_GROUP * tq
    seq = ks0_ref.shape[0]
    n_cmp_pad = kvc_ref.shape[2]
    n_sel = seq // SEL_BLOCK
    scale = HEAD_DIM ** -0.5
    scale2 = scale * LOG2E
    t0 = pl.multiple_of(pl.program_id(1) * tq, tq)
    branch_gate = _sigmoid(small_ref[...])

    tok_col = t0 + lax.broadcasted_iota(jnp.int32, (tq, 1), 0)
    tok_rows = jnp.broadcast_to(tok_col[None], (NSA_GROUP, tq, 1)).reshape(rows, 1)

    for g, (ks_ref, kw_ref, vs_ref, vw_ref) in enumerate(
            ((ks0_ref, kw0_ref, vs0_ref, vw0_ref), (ks1_ref, kw1_ref, vs1_ref, vw1_ref))):
        q4 = jnp.concatenate(
            [q_ref[:, (g * NSA_GROUP + h) * HEAD_DIM:(g * NSA_GROUP + h + 1) * HEAD_DIM] for h in range(NSA_GROUP)],
            axis=0)

        kc = kvc_ref[0, g]
        vc = kvc_ref[0, NSA_KV_HEADS + g]
        cmp_end = lax.broadcasted_iota(jnp.int32, (1, n_cmp_pad), 1) * CMP_STRIDE + (CMP_BLOCK - 1)
        e_c, sum_c = _biased_exp2(_dot_t(q4, kc) * scale2, jnp.where(cmp_end <= tok_col, 0.0, MASKED))
        p_c = e_c * jnp.where(tok_rows >= CMP_BLOCK - 1, 1.0 / sum_c, 0.0)
        o_c = _dot(p_c.astype(BF16), vc)

        p_sum = p_c[0:tq]
        for h in range(1, NSA_GROUP):
            p_sum = p_sum + p_c[h * tq:(h + 1) * tq]
        jj = lax.broadcasted_iota(jnp.int32, (n_sel, n_cmp_pad), 0)
        cc = lax.broadcasted_iota(jnp.int32, (n_sel, n_cmp_pad), 1)
        overlap_t = jnp.where((cc * CMP_STRIDE < (jj + 1) * SEL_BLOCK) & (cc * CMP_STRIDE + CMP_BLOCK > jj * SEL_BLOCK)
                              & (cc < n_cmp_pad - 1), 1.0, 0.0).astype(BF16)
        hi, mid, lo = _split3(p_sum)
        imp_t = (_dot_t(overlap_t, lo) + _dot_t(overlap_t, mid)) + _dot_t(overlap_t, hi)
        blk = lax.broadcasted_iota(jnp.int32, (n_sel, tq), 0)
        cur = (t0 + lax.broadcasted_iota(jnp.int32, (n_sel, tq), 1)) >> SEL_SHIFT
        forced = (blk == 0) | (blk == cur) | (blk == cur - 1)
        score = jnp.where(forced, FORCED_SCORE, jnp.where(blk <= cur, imp_t, INVALID_SCORE))
        sel_bias = _top_rows_bias(score, n_top).T.astype(BF16)

        def sel_tile(kt, carry, causal):
            m, l, acc = carry
            k0 = pl.multiple_of(kt * NSA_TK, NSA_TK)
            blk_of_key = (k0 + lax.broadcasted_iota(jnp.int32, (n_sel, NSA_TK), 1)) >> SEL_SHIFT
            expand = jnp.where(blk_of_key == lax.broadcasted_iota(jnp.int32, (n_sel, NSA_TK), 0), 1.0, 0.0)
            bias = _dot(sel_bias, expand.astype(BF16))
            if causal:
                kpos = k0 + lax.broadcasted_iota(jnp.int32, (1, NSA_TK), 1)
                bias = bias + jnp.where(kpos <= tok_col, 0.0, MASKED)
            y = (_dot_t(q4, ks_ref[pl.ds(k0, NSA_TK), :]) * scale2).reshape(NSA_GROUP, tq, NSA_TK) + bias[None]
            y = y.reshape(rows, NSA_TK)
            m_new = jnp.maximum(m, jnp.max(y, axis=-1, keepdims=True))
            alpha = jnp.exp2(m - m_new)
            e = jnp.exp2(y - m_new)
            l = alpha * l + jnp.sum(e, axis=-1, keepdims=True)
            acc = alpha * acc + _dot(e.astype(BF16), vs_ref[pl.ds(k0, NSA_TK), :])
            return m_new, l, acc

        kd = t0 // NSA_TK
        init = (jnp.full((rows, 1), 3.0 * MASKED, F32), jnp.zeros((rows, 1), F32),
                jnp.zeros((rows, HEAD_DIM), F32))
        carry = lax.fori_loop(0, kd, functools.partial(sel_tile, causal=False), init)
        _, l, acc = sel_tile(kd, carry, True)
        o_s = acc * (1.0 / l)

        w0 = pl.multiple_of(jnp.maximum(t0 - WINDOW, 0), tq)
        wpos = w0 + lax.broadcasted_iota(jnp.int32, (1, WINDOW + tq), 1)
        e_w, sum_w = _biased_exp2(_dot_t(q4, kw_ref[pl.ds(w0, WINDOW + tq), :]) * scale2,
                                  jnp.where((wpos <= tok_col) & (wpos > tok_col - WINDOW), 0.0, MASKED))
        o_w = _dot(e_w.astype(BF16), vw_ref[pl.ds(w0, WINDOW + tq), :]) * (1.0 / sum_w)

        for h in range(NSA_GROUP):
            head = g * NSA_GROUP + h
            r = slice(h * tq, (h + 1) * tq)
            lane = bgate_lane + head * N_BRANCH
            mix = (branch_gate[:, lane:lane + 1] * o_c[r] + branch_gate[:, lane + 1:lane + 2] * o_s[r]
                   + branch_gate[:, lane + 2:lane + 3] * o_w[r])
            cols = slice(head * HEAD_DIM, (head + 1) * HEAD_DIM)
            o_ref[:, cols] = (mix * gate_ref[:, cols]).astype(o_ref.dtype)


def _nsa(zq, zv, kvc, zc, zg, *, batch, seq, cols):
    nq = seq // NSA_TQ
    width = NSA_KV_HEADS * NSA_GROUP * HEAD_DIM
    n_top = min(SEL_TOPK, seq // SEL_BLOCK)
    row = lambda b, i: b * nq + i
    kv_spec = lambda c: pl.BlockSpec((seq, HEAD_DIM), lambda b, i: (b, c))
    return pl.pallas_call(
        functools.partial(_nsa_kernel, n_top=n_top, bgate_lane=cols["bgate_lane"]),
        out_shape=jax.ShapeDtypeStruct((batch * seq, width), BF16),
        grid=(batch, nq),
        in_specs=[pl.BlockSpec((NSA_TQ, width), lambda b, i: (row(b, i), cols["nsa_q"] * HEAD_DIM // width)),
                  pl.BlockSpec((1,) + kvc.shape[1:], lambda b, i: (b, 0, 0, 0)),
                  kv_spec(cols["k_slc"]), kv_spec(cols["k_slc"] + 1),
                  kv_spec(cols["k_win"]), kv_spec(cols["k_win"] + 1),
                  kv_spec(cols["v_slc"]), kv_spec(cols["v_slc"] + 1),
                  kv_spec(cols["v_win"]), kv_spec(cols["v_win"] + 1),
                  pl.BlockSpec((NSA_TQ, LANES), lambda b, i: (row(b, i), cols["small"])),
                  pl.BlockSpec((NSA_TQ, width), lambda b, i: (row(b, i), cols["nsa_gate"] * HEAD_DIM // width))],
        out_specs=pl.BlockSpec((NSA_TQ, width), lambda b, i: (row(b, i), 0)),
        compiler_params=_params("parallel", "arbitrary"),
        name="sparse_attention",
    )(zq, kvc, zq, zq, zq, zq, zv, zv, zv, zv, zc, zg)


FOX_T = 512


def _fox_kernel(q_ref, k_ref, v_ref, c_ref, gate_ref, o_ref):
    t = FOX_T
    scale2 = HEAD_DIM ** -0.5 * LOG2E
    i = pl.program_id(2)
    q = q_ref[...]
    q0 = pl.multiple_of(i * t, t)
    c_tile_end = c_ref[0, :, pl.ds(q0, t)][:, t - 1:t]

    def tile(kt, carry, causal):
        m, l, acc = carry
        k0 = pl.multiple_of(kt * t, t)
        bias = (c_tile_end - c_ref[0, :, pl.ds(k0, t)]) * LOG2E
        y = _dot_t(q, k_ref[pl.ds(k0, t), :]) * scale2 + bias
        if causal:
            qpos = lax.broadcasted_iota(jnp.int32, (t, t), 0)
            kpos = lax.broadcasted_iota(jnp.int32, (t, t), 1)
            y = jnp.where(kpos <= qpos, y, MASKED)
        m_new = jnp.maximum(m, jnp.max(y, axis=-1, keepdims=True))
        alpha = jnp.exp2(m - m_new)
        e = jnp.exp2(y - m_new)
        l = alpha * l + jnp.sum(e, axis=-1, keepdims=True)
        acc = alpha * acc + _dot(e.astype(BF16), v_ref[pl.ds(k0, t), :])
        return m_new, l, acc

    init = (jnp.full((t, 1), MASKED, F32), jnp.zeros((t, 1), F32), jnp.zeros((t, HEAD_DIM), F32))
    carry = lax.fori_loop(0, i, functools.partial(tile, causal=False), init)
    _, l, acc = tile(i, carry, True)
    o_ref[...] = (acc * (1.0 / l) * gate_ref[...]).astype(o_ref.dtype)


def _fox(zq, zv, cpre, zg, *, batch, seq, heads, cols):
    nq = seq // FOX_T
    row = lambda b, h, i: b * nq + i
    return pl.pallas_call(
        _fox_kernel,
        out_shape=jax.ShapeDtypeStruct((batch * seq, heads * HEAD_DIM), BF16),
        grid=(batch, heads, nq),
        in_specs=[pl.BlockSpec((FOX_T, HEAD_DIM), lambda b, h, i: (row(b, h, i), cols["fox_q"] + h)),
                  pl.BlockSpec((seq, HEAD_DIM), lambda b, h, i: (b, cols["fox_k"] + h)),
                  pl.BlockSpec((seq, HEAD_DIM), lambda b, h, i: (b, cols["fox_v"] + h)),
                  pl.BlockSpec((1, 1, seq), lambda b, h, i: (b * heads + h, 0, 0)),
                  pl.BlockSpec((FOX_T, HEAD_DIM), lambda b, h, i: (row(b, h, i), cols["fox_gate"] + h))],
        out_specs=pl.BlockSpec((FOX_T, HEAD_DIM), lambda b, h, i: (row(b, h, i), h)),
        compiler_params=_params("parallel", "parallel", "arbitrary"),
        name="forgetting_attention",
    )(zq, zq, zv, cpre.reshape(batch * heads, 1, seq), zg)


def _merge_kernel(ya_ref, yb_ref, yc_ref, w_ref, ga_ref, gb_ref, gc_ref, o_ref):
    merged = ga_ref[...] * _dot(ya_ref[...], w_ref[0])
    merged = merged + gb_ref[...] * _dot(yb_ref[...], w_ref[1])
    merged = merged + gc_ref[...] * _dot(yc_ref[...], w_ref[2])
    o_ref[...] = merged.astype(o_ref.dtype)


def _merge(ya, yb, yc, w_branch, zm, *, tm, tn):
    m, k = ya.shape
    n = w_branch.shape[2]
    y_spec = pl.BlockSpec((tm, k), lambda i, j: (i, 0))
    g_spec = lambda br: pl.BlockSpec((tm, tn), lambda i, j: (i, br * (n // tn) + j))
    return pl.pallas_call(
        _merge_kernel,
        out_shape=jax.ShapeDtypeStruct((m, n), BF16),
        grid=(m // tm, n // tn),
        in_specs=[y_spec, y_spec, y_spec, pl.BlockSpec((N_BRANCH, k, tn), lambda i, j: (0, 0, j)),
                  g_spec(0), g_spec(1), g_spec(2)],
        out_specs=pl.BlockSpec((tm, tn), lambda i, j: (i, j)),
        compiler_params=_params("parallel", "arbitrary"),
        name="merge",
    )(ya, yb, yc, w_branch, zm, zm, zm)


def _ple_kernel(h_ref, p_ref, wp_ref, pg_ref, wg_ref, ng_ref, h_out_ref, hn_out_ref):
    h = h_ref[...]
    e = _rms(_dot(p_ref[...].astype(BF16), wp_ref[...]), pg_ref[...])
    h = h + _sigmoid(_dot(h.astype(BF16), wg_ref[...])) * e
    h_out_ref[...] = h
    hn_out_ref[...] = _rms(h, ng_ref[...]).astype(BF16)


def _ple(h, p, w_ple, ple_gain, w_gate, next_gain, *, tm):
    t, d = h.shape
    pd = p.shape[1]
    const = lambda i: (0, 0)
    return pl.pallas_call(
        _ple_kernel,
        out_shape=(jax.ShapeDtypeStruct((t, d), F32), jax.ShapeDtypeStruct((t, d), BF16)),
        grid=(t // tm,),
        in_specs=[pl.BlockSpec((tm, d), lambda i: (i, 0)), pl.BlockSpec((tm, pd), lambda i: (i, 0)),
                  pl.BlockSpec((pd, d), const), pl.BlockSpec((1, d), const),
                  pl.BlockSpec((d, d), const), pl.BlockSpec((1, d), const)],
        out_specs=(pl.BlockSpec((tm, d), lambda i: (i, 0)), pl.BlockSpec((tm, d), lambda i: (i, 0))),
        compiler_params=_params("parallel"),
        name="ple_update",
    )(h, p, w_ple, ple_gain, w_gate, next_gain)


def _in_proj_layout(d_model, branch_width):
    kvw = NSA_KV_HEADS * HEAD_DIM
    heads = branch_width // HEAD_DIM
    splits = (("lru_x", branch_width), ("lru_gate", branch_width), ("nsa_q", branch_width),
              ("nsa_k_cmp", kvw), ("nsa_v_cmp", kvw), ("nsa_k_slc", kvw), ("nsa_v_slc", kvw),
              ("nsa_k_win", kvw), ("nsa_v_win", kvw), ("nsa_bgate", heads * N_BRANCH),
              ("nsa_gate", branch_width), ("fox_q", branch_width), ("fox_k", branch_width),
              ("fox_v", branch_width), ("fox_f", heads), ("fox_gate", branch_width),
              ("merge", N_BRANCH * d_model))
    offs, start = {}, 0
    for name, size in splits:
        offs[name] = (start, size)
        start += size
    return offs


_GROUPS = {
    "za": (("lru_x",), "none", F32),
    "zg": (("lru_gate", "nsa_gate", "fox_gate"), "silu", F32),
    "zq": (("nsa_q", "nsa_k_slc", "nsa_k_win", "fox_q", "fox_k"), "headnorm", BF16),
    "zv": (("nsa_v_slc", "nsa_v_win", "fox_v"), "none", BF16),
    "zc": (("nsa_k_cmp", "nsa_v_cmp", "fox_f", "nsa_bgate"), "none", F32),
    "zm": (("merge",), "sigmoid", F32),
}


def _gather_cols(w, offs, names, pad_to=None):
    parts = [lax.slice_in_dim(w, offs[n][0], offs[n][0] + offs[n][1], axis=w.ndim - 1) for n in names]
    out = jnp.concatenate(parts, axis=-1) if len(parts) > 1 else parts[0]
    if pad_to is not None and out.shape[-1] % pad_to:
        pad = pad_to - out.shape[-1] % pad_to
        out = jnp.pad(out, [(0, 0)] * (out.ndim - 1) + [(0, pad)])
    return out


def _block_diag(w, pack):
    depth, nb, k, _ = w.shape
    per = pack // k
    w = w.reshape(depth, nb // per, per, k, k)
    eye = jnp.eye(per, dtype=w.dtype)
    return jnp.einsum("dgpij,pq->dgpiqj", w, eye).reshape(depth, nb // per, pack, pack)


def _tile(n, target):
    if n <= target:
        return n
    best = LANES
    for t in range(LANES, target + 1, LANES):
        if n % t == 0:
            best = t
    return best


def kernel(x, p, ln_gain, w_in, b_in, conv_w, conv_b, lru_wa, lru_ba, lru_wx, lru_bx, lru_lambda, cmp_w1, cmp_w2,
           cmp_pos, nsa_q_gain, nsa_k_gain, fox_q_gain, fox_k_gain, w_branch, w_out, w_ple, ple_gain, w_ple_gate):
    batch, seq, d_model = x.shape
    depth = w_in.shape[0]
    bw = conv_w.shape[2]
    heads = bw // HEAD_DIM
    t = batch * seq
    assert seq % NSA_TK == 0 and seq % FOX_T == 0 and seq >= WINDOW + NSA_TQ
    offs = _in_proj_layout(d_model, bw)

    wg, bg = {}, {}
    for name, (members, _, _) in _GROUPS.items():
        wg[name] = _gather_cols(w_in, offs, members, pad_to=LANES).astype(BF16)
        bg[name] = _gather_cols(b_in, offs, members, pad_to=LANES)[:, None, :]
    rep = lambda g, n: jnp.tile(g, (1, n))
    zq_gain = jnp.concatenate([rep(nsa_q_gain, heads), rep(nsa_k_gain, 2 * NSA_KV_HEADS),
                               rep(fox_q_gain, heads), rep(fox_k_gain, heads)], axis=1)[:, None, :]
    hb = bw // HEAD_DIM
    cols = {"nsa_q": 0, "k_slc": hb, "k_win": hb + NSA_KV_HEADS, "fox_q": hb + 2 * NSA_KV_HEADS,
            "fox_k": 2 * hb + 2 * NSA_KV_HEADS,
            "v_slc": 0, "v_win": NSA_KV_HEADS, "fox_v": 2 * NSA_KV_HEADS,
            "small": 2 * NSA_KV_HEADS, "bgate_lane": heads,
            "nsa_gate": hb, "fox_gate": 2 * hb}
    wa_bd = _block_diag(lru_wa, LRU_PACK).astype(BF16)
    wx_bd = _block_diag(lru_wx, LRU_PACK).astype(BF16)
    w_branch_b = w_branch.astype(BF16)
    w_out_b = w_out.astype(BF16)
    w_ple_b = w_ple.astype(BF16)
    w_gate_b = w_ple_gate.astype(BF16)
    cmp_w1_b = cmp_w1.astype(BF16)
    cmp_w2_b = cmp_w2.astype(BF16)
    zero_bias = jnp.zeros((1, d_model), F32)

    tm = _tile(t, 1024)
    h = x.reshape(t, d_model)
    p2 = p.reshape(depth, t, p.shape[-1])
    hn = _prenorm(h, ln_gain[0][None], _tile(t, 512))
    for l in range(depth):
        z = {}
        for name, (_, epilogue, dtype) in _GROUPS.items():
            n = wg[name].shape[2]
            z[name] = _matmul(hn, wg[name][l], bg[name][l], epilogue=epilogue, out_dtype=dtype, tm=tm,
                              tn=_tile(n, 640), extra=zq_gain[l] if epilogue == "headnorm" else None,
                              name="in_proj_" + name)
        ya = _rglru(z["za"], z["zg"], conv_w[l], conv_b[l][None], wa_bd[l], lru_ba[l][None], wx_bd[l],
                    lru_bx[l][None], lru_lambda[l][None], batch=batch, seq=seq, ts=_tile(seq, 512))
        kvc = _compress(z["zc"], cmp_pos[l], cmp_w1_b[l], cmp_w2_b[l], nsa_k_gain[l][None], batch=batch, seq=seq)
        yb = _nsa(z["zq"], z["zv"], kvc, z["zc"], z["zg"], batch=batch, seq=seq, cols=cols)
        cpre = _forget_prefix(z["zc"], batch=batch, seq=seq, heads=heads, col_block=cols["small"])
        yc = _fox(z["zq"], z["zv"], cpre, z["zg"], batch=batch, seq=seq, heads=heads, cols=cols)
        merged = _merge(ya, yb, yc, w_branch_b[l], z["zm"], tm=tm, tn=_tile(d_model, 512))
        h = _matmul(merged, w_out_b[l], zero_bias, epilogue="residual", out_dtype=F32, tm=tm,
                    tn=_tile(d_model, 512), extra=h, name="out_proj")
        next_gain = ln_gain[(l + 1) % depth][None]
        h, hn = _ple(h, p2[l], w_ple_b[l], ple_gain[l][None], w_gate_b[l], next_gain, tm=_tile(t, 512))
    return h.reshape(batch, seq, d_model)
```

```python
import functools

import jax
import jax.numpy as jnp
from jax import lax
from jax.experimental import pallas as pl
from jax.experimental.pallas import tpu as pltpu

F32 = jnp.float32
BF16 = jnp.bfloat16

HEAD_DIM = 128
NORM_EPS = 1e-6
LRU_BLOCK = 64
CONV_WIDTH = 4
LRU_C = 8.0
NSA_KV_HEADS = 2
NSA_GROUP = 4
CMP_BLOCK = 32
CMP_STRIDE = 16
SEL_BLOCK = 64
SEL_TOPK = 16
WINDOW = 512
N_BRANCH = 3

V7X_VMEM_BYTES = 64 * 1024 * 1024
VMEM_LIMIT = V7X_VMEM_BYTES - 8 * 1024 * 1024
LANES = 128
SUBLANES = 8
MXU_WIDTH = 256
LOG2E = 1.4426950408889634

MASKED = -1e30
FORCED_SCORE = 1e30
INVALID_SCORE = -1.0


def _params(*semantics):
    return pltpu.CompilerParams(dimension_semantics=semantics, vmem_limit_bytes=VMEM_LIMIT)


def _dot(a, b):
    return jnp.dot(a, b, preferred_element_type=F32)


def _dot_t(a, b):
    return lax.dot_general(a, b, (((1,), (1,)), ((), ())), preferred_element_type=F32)


def _split3(x):
    hi = x.astype(BF16)
    r1 = x - hi.astype(F32)
    mid = r1.astype(BF16)
    lo = (r1 - mid.astype(F32)).astype(BF16)
    return hi, mid, lo


def _sigmoid(x):
    return 0.5 * jnp.tanh(0.5 * x) + 0.5


def _softplus(x):
    return jnp.maximum(x, 0.0) + jnp.log(1.0 + jnp.exp(-jnp.abs(x)))


def _rms(x, gain):
    y = x * lax.rsqrt(jnp.mean(x * x, axis=-1, keepdims=True) + NORM_EPS)
    return y * gain


def _prenorm_kernel(h_ref, g_ref, o_ref):
    o_ref[...] = _rms(h_ref[...], g_ref[...]).astype(BF16)


def _prenorm(h, gain, tm):
    t, d = h.shape
    return pl.pallas_call(
        _prenorm_kernel,
        out_shape=jax.ShapeDtypeStruct((t, d), BF16),
        grid=(t // tm,),
        in_specs=[pl.BlockSpec((tm, d), lambda i: (i, 0)), pl.BlockSpec((1, d), lambda i: (0, 0))],
        out_specs=pl.BlockSpec((tm, d), lambda i: (i, 0)),
        compiler_params=_params("parallel"),
        name="prenorm",
    )(h, gain)


def _matmul_kernel(*refs, epilogue):
    if epilogue in ("headnorm", "residual"):
        x_ref, w_ref, b_ref, e_ref, o_ref = refs
    else:
        x_ref, w_ref, b_ref, o_ref = refs
    tn = o_ref.shape[1]
    step = MXU_WIDTH if tn % MXU_WIDTH == 0 else tn
    x = x_ref[...]
    for c0 in range(0, tn, step):
        cols = slice(c0, c0 + step)
        acc = _dot(x, w_ref[:, cols]) + b_ref[:, cols]
        if epilogue == "silu":
            acc = acc * _sigmoid(acc)
        elif epilogue == "sigmoid":
            acc = _sigmoid(acc)
        elif epilogue == "residual":
            acc = acc + e_ref[:, cols]
        if epilogue == "headnorm":
            for h0 in range(c0, c0 + step, HEAD_DIM):
                head = slice(h0, h0 + HEAD_DIM)
                o_ref[:, head] = _rms(acc[:, h0 - c0:h0 - c0 + HEAD_DIM], e_ref[:, head]).astype(o_ref.dtype)
        else:
            o_ref[:, cols] = acc.astype(o_ref.dtype)


def _matmul(x, w, b, *, epilogue, out_dtype, tm, tn, extra=None, name):
    m, k = x.shape
    n = w.shape[1]
    in_specs = [pl.BlockSpec((tm, k), lambda i, j: (i, 0)),
                pl.BlockSpec((k, tn), lambda i, j: (0, j)),
                pl.BlockSpec((1, tn), lambda i, j: (0, j))]
    args = [x, w, b]
    if epilogue == "headnorm":
        in_specs.append(pl.BlockSpec((1, tn), lambda i, j: (0, j)))
        args.append(extra)
    elif epilogue == "residual":
        in_specs.append(pl.BlockSpec((tm, tn), lambda i, j: (i, j)))
        args.append(extra)
    return pl.pallas_call(
        functools.partial(_matmul_kernel, epilogue=epilogue),
        out_shape=jax.ShapeDtypeStruct((m, n), out_dtype),
        grid=(m // tm, n // tn),
        in_specs=in_specs,
        out_specs=pl.BlockSpec((tm, tn), lambda i, j: (i, j)),
        compiler_params=_params("parallel", "arbitrary"),
        name=name,
    )(*args)


LRU_PACK = 256


def _rglru_kernel(u_ref, g_ref, cw_ref, cb_ref, wa_ref, ba_ref, wx_ref, bx_ref, lam_ref, o_ref,
                  ubuf, abuf, bbuf, hbuf, hcar):
    ts, w = u_ref.shape
    s_idx = pl.program_id(1)

    @pl.when(s_idx == 0)
    def _():
        ubuf[0:8, :] = jnp.zeros((8, w), F32)
        hcar[...] = jnp.zeros((1, w), F32)

    ubuf[8:, :] = u_ref[...]
    uc = cb_ref[...] + cw_ref[CONV_WIDTH - 1:CONV_WIDTH, :] * ubuf[8:, :]
    for d in range(1, CONV_WIDTH):
        uc = uc + cw_ref[CONV_WIDTH - 1 - d:CONV_WIDTH - d, :] * ubuf[pl.ds(8 - d, ts), :]
    ubuf[0:8, :] = ubuf[ts:ts + 8, :]

    ucb = uc.astype(BF16)
    neg_c_softplus = -LRU_C * _softplus(-lam_ref[...])
    for c in range(w // LRU_PACK):
        cols = slice(c * LRU_PACK, (c + 1) * LRU_PACK)
        rec = _sigmoid(_dot(ucb[:, cols], wa_ref[c]) + ba_ref[:, cols])
        inp = _sigmoid(_dot(ucb[:, cols], wx_ref[c]) + bx_ref[:, cols])
        log_a = rec * neg_c_softplus[:, cols]
        a = jnp.exp(log_a)
        th = jnp.tanh(log_a)
        one_minus_a2 = -2.0 * th / (1.0 - th)
        abuf[:, cols] = a
        bbuf[:, cols] = jnp.sqrt(one_minus_a2) * (inp * uc[:, cols])

    def step(t, h):
        h = abuf[pl.ds(t, 1), :] * h + bbuf[pl.ds(t, 1), :]
        hbuf[pl.ds(t, 1), :] = h
        return h

    hcar[...] = lax.fori_loop(0, ts, step, hcar[...], unroll=8)
    o_ref[...] = (hbuf[...] * g_ref[...]).astype(o_ref.dtype)


def _rglru(za, zg, conv_w, conv_b, wa_bd, ba, wx_bd, bx, lam, *, batch, seq, ts):
    w = conv_w.shape[1]
    ns = seq // ts
    row = lambda b, s: (b * ns + s, 0)
    const = lambda b, s: (0, 0)
    return pl.pallas_call(
        _rglru_kernel,
        out_shape=jax.ShapeDtypeStruct((batch * seq, w), BF16),
        grid=(batch, ns),
        in_specs=[pl.BlockSpec((ts, w), row), pl.BlockSpec((ts, w), row),
                  pl.BlockSpec((CONV_WIDTH, w), const), pl.BlockSpec((1, w), const),
                  pl.BlockSpec(wa_bd.shape, lambda b, s: (0, 0, 0)), pl.BlockSpec((1, w), const),
                  pl.BlockSpec(wx_bd.shape, lambda b, s: (0, 0, 0)), pl.BlockSpec((1, w), const),
                  pl.BlockSpec((1, w), const)],
        out_specs=pl.BlockSpec((ts, w), row),
        scratch_shapes=[pltpu.VMEM((ts + 8, w), F32), pltpu.VMEM((ts, w), F32), pltpu.VMEM((ts, w), F32),
                        pltpu.VMEM((ts, w), F32), pltpu.VMEM((1, w), F32)],
        compiler_params=_params("parallel", "arbitrary"),
        name="rglru",
    )(za, zg, conv_w, conv_b, wa_bd, ba, wx_bd, bx, lam)


def _compress_kernel(x_ref, pos_ref, w1_ref, w2_ref, gain_ref, o_ref, xpad):
    seq = x_ref.shape[0]
    n_out = seq // CMP_STRIDE
    j = pl.program_id(1)
    xpad[0:seq, :] = x_ref[...]
    xpad[seq:, :] = jnp.zeros((CMP_BLOCK, HEAD_DIM), F32)
    hidden = jnp.zeros((n_out, w1_ref.shape[2]), F32)
    for r in range(CMP_BLOCK):
        rows = xpad[pl.ds(r, n_out, stride=CMP_STRIDE), :] + pos_ref[0, r:r + 1, :]
        hidden = hidden + _dot(rows.astype(BF16), w1_ref[0, r * HEAD_DIM:(r + 1) * HEAD_DIM, :])
    hidden = hidden * _sigmoid(hidden)
    out = _dot(hidden.astype(BF16), w2_ref[0])
    out = jnp.where(j < NSA_KV_HEADS, _rms(out, gain_ref[...]), out)
    row = lax.broadcasted_iota(jnp.int32, out.shape, 0)
    o_ref[0, 0] = jnp.where(row < n_out - 1, out, 0.0).astype(o_ref.dtype)


def _compress(zc, cmp_pos, cmp_w1, cmp_w2, k_gain, *, batch, seq):
    n_out = seq // CMP_STRIDE
    hid = cmp_w1.shape[2]
    return pl.pallas_call(
        _compress_kernel,
        out_shape=jax.ShapeDtypeStruct((batch, 2 * NSA_KV_HEADS, n_out, HEAD_DIM), BF16),
        grid=(batch, 2 * NSA_KV_HEADS),
        in_specs=[pl.BlockSpec((seq, HEAD_DIM), lambda b, j: (b, j)),
                  pl.BlockSpec((1, CMP_BLOCK, HEAD_DIM), lambda b, j: (j // NSA_KV_HEADS, 0, 0)),
                  pl.BlockSpec((1, CMP_BLOCK * HEAD_DIM, hid), lambda b, j: (j // NSA_KV_HEADS, 0, 0)),
                  pl.BlockSpec((1, hid, HEAD_DIM), lambda b, j: (j // NSA_KV_HEADS, 0, 0)),
                  pl.BlockSpec((1, HEAD_DIM), lambda b, j: (0, 0))],
        out_specs=pl.BlockSpec((1, 1, n_out, HEAD_DIM), lambda b, j: (b, j, 0, 0)),
        scratch_shapes=[pltpu.VMEM((seq + CMP_BLOCK, HEAD_DIM), F32)],
        compiler_params=_params("parallel", "arbitrary"),
        name="compress",
    )(zc, cmp_pos, cmp_w1, cmp_w2, k_gain)


FOX_BIAS_TERMS = 3


def _forget_prefix_kernel(s_ref, o_ref, *, heads):
    seq = s_ref.shape[0]
    f = s_ref[...]
    log_f = jnp.minimum(f, 0.0) - jnp.log(1.0 + jnp.exp(-jnp.abs(f)))
    r = lax.broadcasted_iota(jnp.int32, (LANES, LANES), 0)
    c = lax.broadcasted_iota(jnp.int32, (LANES, LANES), 1)
    lower = jnp.where(c <= r, 1.0, 0.0).astype(BF16)
    pr = lax.broadcasted_iota(jnp.int32, (LANES, heads * LANES), 0)
    pc = lax.broadcasted_iota(jnp.int32, (LANES, heads * LANES), 1)
    place = [jnp.where((pc == pr * LANES + j) & (pr < heads), 1.0, 0.0).astype(BF16) for j in range(FOX_BIAS_TERMS)]
    carry = jnp.zeros((1, LANES), F32)
    for k in range(seq // LANES):
        rows = slice(k * LANES, (k + 1) * LANES)
        hi, mid, lo = _split3(log_f[rows])
        chunk = (_dot(lower, lo) + _dot(lower, mid)) + _dot(lower, hi) + carry
        carry = chunk[LANES - 1:LANES, :]
        terms = _split3(chunk * (-LOG2E))
        out = _dot(terms[0], place[0])
        for j in range(1, FOX_BIAS_TERMS):
            out = out + _dot(terms[j], place[j])
        o_ref[rows, :] = out.astype(o_ref.dtype)


def _forget_prefix(zc, *, batch, seq, heads, col_block):
    return pl.pallas_call(
        functools.partial(_forget_prefix_kernel, heads=heads),
        out_shape=jax.ShapeDtypeStruct((batch * seq, heads * LANES), BF16),
        grid=(batch,),
        in_specs=[pl.BlockSpec((seq, LANES), lambda b: (b, col_block))],
        out_specs=pl.BlockSpec((seq, heads * LANES), lambda b: (b, 0)),
        compiler_params=_params("parallel"),
        name="forget_prefix",
    )(zc)


NSA_TQ = 256
NSA_TK = 512
SEL_SHIFT = SEL_BLOCK.bit_length() - 1


def _biased_exp2(s, bias):
    rows, k = s.shape
    y = (s.reshape(NSA_GROUP, rows // NSA_GROUP, k) + bias[None]).reshape(rows, k)
    e = jnp.exp2(y - jnp.max(y, axis=-1, keepdims=True))
    return e, jnp.sum(e, axis=-1, keepdims=True)


def _top_rows_bias(score, n_top):
    n, t = score.shape
    n_groups = n // SUBLANES
    groups = [score[r * SUBLANES:(r + 1) * SUBLANES] for r in range(n_groups)]
    ranks = [jnp.zeros((SUBLANES, t), F32) for _ in range(n_groups)]
    local = lax.broadcasted_iota(jnp.int32, (SUBLANES, t), 0)
    for i in range(n):
        gi, li = divmod(i, SUBLANES)
        s_i = groups[gi][li:li + 1, :]
        for r in range(n_groups):
            if r > gi:
                ahead = s_i >= groups[r]
            elif r < gi:
                ahead = s_i > groups[r]
            else:
                ahead = (s_i > groups[r]) | ((s_i == groups[r]) & (local > li))
            ranks[r] = ranks[r] + jnp.where(ahead, 1.0, 0.0)
    return jnp.where(jnp.concatenate(ranks, axis=0) < n_top, 0.0, MASKED)


def _nsa_kernel(q_ref, kvc_ref, ks0_ref, ks1_ref, kw0_ref, kw1_ref, vs0_ref, vs1_ref, vw0_ref, vw1_ref,
                small_ref, gate_ref, o_ref, ks_aug, *, n_top, bgate_lane):
    tq = NSA_TQ
    rows = NSA_GROUP * tq
    seq = ks0_ref.shape[0]
    n_cmp_pad = kvc_ref.shape[2]
    n_sel = seq // SEL_BLOCK
    t0 = pl.multiple_of(pl.program_id(1) * tq, tq)
    branch_gate = _sigmoid(small_ref[...])

    @pl.when(pl.program_id(1) == 0)
    def _():
        blk_of_key = lax.broadcasted_iota(jnp.int32, (seq, LANES), 0) >> SEL_SHIFT
        one_hot = jnp.where(blk_of_key == lax.broadcasted_iota(jnp.int32, (seq, LANES), 1), 1.0, 0.0).astype(BF16)
        for g, ks_ref in enumerate((ks0_ref, ks1_ref)):
            ks_aug[g, :, 0:HEAD_DIM] = ks_ref[...]
            ks_aug[g, :, HEAD_DIM:] = one_hot

    tok_col = t0 + lax.broadcasted_iota(jnp.int32, (tq, 1), 0)
    tok_rows = jnp.broadcast_to(tok_col[None], (NSA_GROUP, tq, 1)).reshape(rows, 1)

    for g, (ks_ref, kw_ref, vs_ref, vw_ref) in enumerate(
            ((ks0_ref, kw0_ref, vs0_ref, vw0_ref), (ks1_ref, kw1_ref, vs1_ref, vw1_ref))):
        q4 = jnp.concatenate(
            [q_ref[:, (g * NSA_GROUP + h) * HEAD_DIM:(g * NSA_GROUP + h + 1) * HEAD_DIM] for h in range(NSA_GROUP)],
            axis=0)

        kc = kvc_ref[0, g]
        vc = kvc_ref[0, NSA_KV_HEADS + g]
        cmp_end = lax.broadcasted_iota(jnp.int32, (1, n_cmp_pad), 1) * CMP_STRIDE + (CMP_BLOCK - 1)
        e_c, sum_c = _biased_exp2(_dot_t(q4, kc), jnp.where(cmp_end <= tok_col, 0.0, MASKED))
        p_c = e_c * jnp.where(tok_rows >= CMP_BLOCK - 1, 1.0 / sum_c, 0.0)
        o_c = _dot(p_c.astype(BF16), vc)

        p_sum = p_c[0:tq]
        for h in range(1, NSA_GROUP):
            p_sum = p_sum + p_c[h * tq:(h + 1) * tq]
        jj = lax.broadcasted_iota(jnp.int32, (n_sel, n_cmp_pad), 0)
        cc = lax.broadcasted_iota(jnp.int32, (n_sel, n_cmp_pad), 1)
        overlap_t = jnp.where((cc * CMP_STRIDE < (jj + 1) * SEL_BLOCK) & (cc * CMP_STRIDE + CMP_BLOCK > jj * SEL_BLOCK)
                              & (cc < n_cmp_pad - 1), 1.0, 0.0).astype(BF16)
        hi, mid, lo = _split3(p_sum)
        imp_t = (_dot_t(overlap_t, lo) + _dot_t(overlap_t, mid)) + _dot_t(overlap_t, hi)
        blk = lax.broadcasted_iota(jnp.int32, (n_sel, tq), 0)
        cur = (t0 + lax.broadcasted_iota(jnp.int32, (n_sel, tq), 1)) >> SEL_SHIFT
        forced = (blk == 0) | (blk == cur) | (blk == cur - 1)
        score = jnp.where(forced, FORCED_SCORE, jnp.where(blk <= cur, imp_t, INVALID_SCORE))
        sel_bias = _top_rows_bias(score, n_top)
        sel_bias = jnp.concatenate([sel_bias, jnp.zeros((LANES - n_sel, tq), F32)], axis=0).T.astype(BF16)
        q_aug = jnp.concatenate([q4, jnp.concatenate([sel_bias] * NSA_GROUP, axis=0)], axis=1)

        def sel_tile(kt, carry, causal):
            m, l, acc = carry
            k0 = pl.multiple_of(kt * NSA_TK, NSA_TK)
            y = _dot_t(q_aug, ks_aug[g, pl.ds(k0, NSA_TK), :])
            if causal:
                kpos = k0 + lax.broadcasted_iota(jnp.int32, (1, NSA_TK), 1)
                y = (y.reshape(NSA_GROUP, tq, NSA_TK) + jnp.where(kpos <= tok_col, 0.0, MASKED)[None]).reshape(
                    rows, NSA_TK)
            m_new = jnp.maximum(m, jnp.max(y, axis=-1, keepdims=True))
            alpha = jnp.exp2(m - m_new)
            e = jnp.exp2(y - m_new)
            l = alpha * l + jnp.sum(e, axis=-1, keepdims=True)
            acc = alpha * acc + _dot(e.astype(BF16), vs_ref[pl.ds(k0, NSA_TK), :])
            return m_new, l, acc

        kd = t0 // NSA_TK
        init = (jnp.full((rows, 1), 3.0 * MASKED, F32), jnp.zeros((rows, 1), F32),
                jnp.zeros((rows, HEAD_DIM), F32))
        carry = lax.fori_loop(0, kd, functools.partial(sel_tile, causal=False), init)
        _, l, acc = sel_tile(kd, carry, True)
        o_s = acc * (1.0 / l)

        w0 = pl.multiple_of(jnp.maximum(t0 - WINDOW, 0), tq)
        wpos = w0 + lax.broadcasted_iota(jnp.int32, (1, WINDOW + tq), 1)
        e_w, sum_w = _biased_exp2(_dot_t(q4, kw_ref[pl.ds(w0, WINDOW + tq), :]),
                                  jnp.where((wpos <= tok_col) & (wpos > tok_col - WINDOW), 0.0, MASKED))
        o_w = _dot(e_w.astype(BF16), vw_ref[pl.ds(w0, WINDOW + tq), :]) * (1.0 / sum_w)

        for h in range(NSA_GROUP):
            head = g * NSA_GROUP + h
            r = slice(h * tq, (h + 1) * tq)
            lane = bgate_lane + head * N_BRANCH
            mix = (branch_gate[:, lane:lane + 1] * o_c[r] + branch_gate[:, lane + 1:lane + 2] * o_s[r]
                   + branch_gate[:, lane + 2:lane + 3] * o_w[r])
            cols = slice(head * HEAD_DIM, (head + 1) * HEAD_DIM)
            o_ref[:, cols] = (mix * gate_ref[:, cols]).astype(o_ref.dtype)


def _nsa(zq, zv, kvc, zc, zg, *, batch, seq, cols):
    nq = seq // NSA_TQ
    width = NSA_KV_HEADS * NSA_GROUP * HEAD_DIM
    n_top = min(SEL_TOPK, seq // SEL_BLOCK)
    row = lambda b, i: b * nq + i
    kv_spec = lambda c: pl.BlockSpec((seq, HEAD_DIM), lambda b, i: (b, c))
    return pl.pallas_call(
        functools.partial(_nsa_kernel, n_top=n_top, bgate_lane=cols["bgate_lane"]),
        out_shape=jax.ShapeDtypeStruct((batch * seq, width), BF16),
        grid=(batch, nq),
        in_specs=[pl.BlockSpec((NSA_TQ, width), lambda b, i: (row(b, i), cols["nsa_q"] * HEAD_DIM // width)),
                  pl.BlockSpec((1,) + kvc.shape[1:], lambda b, i: (b, 0, 0, 0)),
                  kv_spec(cols["k_slc"]), kv_spec(cols["k_slc"] + 1),
                  kv_spec(cols["k_win"]), kv_spec(cols["k_win"] + 1),
                  kv_spec(cols["v_slc"]), kv_spec(cols["v_slc"] + 1),
                  kv_spec(cols["v_win"]), kv_spec(cols["v_win"] + 1),
                  pl.BlockSpec((NSA_TQ, LANES), lambda b, i: (row(b, i), cols["small"])),
                  pl.BlockSpec((NSA_TQ, width), lambda b, i: (row(b, i), cols["nsa_gate"] * HEAD_DIM // width))],
        out_specs=pl.BlockSpec((NSA_TQ, width), lambda b, i: (row(b, i), 0)),
        scratch_shapes=[pltpu.VMEM((NSA_KV_HEADS, seq, HEAD_DIM + LANES), BF16)],
        compiler_params=_params("parallel", "arbitrary"),
        name="sparse_attention",
    )(zq, kvc, zq, zq, zq, zq, zv, zv, zv, zv, zc, zg)


FOX_T = 1024
FOX_TK = 512


def _fox_kernel(q_ref, k_ref, kb_ref, v_ref, gate_ref, o_ref, k_aug):
    t, tk = FOX_T, FOX_TK
    i = pl.program_id(2)
    q0 = i * t

    @pl.when(i == 0)
    def _():
        k_aug[:, 0:HEAD_DIM] = k_ref[...]
        k_aug[:, HEAD_DIM:] = kb_ref[...]

    lane = lax.broadcasted_iota(jnp.int32, (t, LANES), 1)
    q_aug = jnp.concatenate([q_ref[...], jnp.where(lane < FOX_BIAS_TERMS, 1.0, 0.0).astype(BF16)], axis=1)

    def tile(kt, carry, causal):
        m, l, acc = carry
        k0 = pl.multiple_of(kt * tk, tk)
        y = _dot_t(q_aug, k_aug[pl.ds(k0, tk), :])
        if causal:
            qpos = q0 + lax.broadcasted_iota(jnp.int32, (t, tk), 0)
            kpos = k0 + lax.broadcasted_iota(jnp.int32, (t, tk), 1)
            y = jnp.where(kpos <= qpos, y, MASKED)
        m_new = jnp.maximum(m, jnp.max(y, axis=-1, keepdims=True))
        alpha = jnp.exp2(m - m_new)
        e = jnp.exp2(y - m_new)
        l = alpha * l + jnp.sum(e, axis=-1, keepdims=True)
        acc = alpha * acc + _dot(e.astype(BF16), v_ref[pl.ds(k0, tk), :])
        return m_new, l, acc

    n_full = q0 // tk
    carry = (jnp.full((t, 1), MASKED, F32), jnp.zeros((t, 1), F32), jnp.zeros((t, HEAD_DIM), F32))
    carry = lax.fori_loop(0, n_full, functools.partial(tile, causal=False), carry)
    for d in range(t // tk):
        carry = tile(n_full + d, carry, True)
    _, l, acc = carry
    o_ref[...] = (acc * (1.0 / l) * gate_ref[...]).astype(o_ref.dtype)


def _fox(zq, zv, kbias, zg, *, batch, seq, heads, cols):
    nq = seq // FOX_T
    row = lambda b, h, i: b * nq + i
    return pl.pallas_call(
        _fox_kernel,
        out_shape=jax.ShapeDtypeStruct((batch * seq, heads * HEAD_DIM), BF16),
        grid=(batch, heads, nq),
        in_specs=[pl.BlockSpec((FOX_T, HEAD_DIM), lambda b, h, i: (row(b, h, i), cols["fox_q"] + h)),
                  pl.BlockSpec((seq, HEAD_DIM), lambda b, h, i: (b, cols["fox_k"] + h)),
                  pl.BlockSpec((seq, LANES), lambda b, h, i: (b, h)),
                  pl.BlockSpec((seq, HEAD_DIM), lambda b, h, i: (b, cols["fox_v"] + h)),
                  pl.BlockSpec((FOX_T, HEAD_DIM), lambda b, h, i: (row(b, h, i), cols["fox_gate"] + h))],
        out_specs=pl.BlockSpec((FOX_T, HEAD_DIM), lambda b, h, i: (row(b, h, i), h)),
        scratch_shapes=[pltpu.VMEM((seq, HEAD_DIM + LANES), BF16)],
        compiler_params=_params("parallel", "parallel", "arbitrary"),
        name="forgetting_attention",
    )(zq, zq, kbias, zv, zg)


def _merge_kernel(ya_ref, yb_ref, yc_ref, w_ref, ga_ref, gb_ref, gc_ref, o_ref):
    tn = o_ref.shape[1]
    step = MXU_WIDTH if tn % MXU_WIDTH == 0 else tn
    ya, yb, yc = ya_ref[...], yb_ref[...], yc_ref[...]
    for c0 in range(0, tn, step):
        cols = slice(c0, c0 + step)
        merged = ga_ref[:, cols] * _dot(ya, w_ref[0, :, cols])
        merged = merged + gb_ref[:, cols] * _dot(yb, w_ref[1, :, cols])
        merged = merged + gc_ref[:, cols] * _dot(yc, w_ref[2, :, cols])
        o_ref[:, cols] = merged.astype(o_ref.dtype)


def _merge(ya, yb, yc, w_branch, zm, *, tm, tn):
    m, k = ya.shape
    n = w_branch.shape[2]
    y_spec = pl.BlockSpec((tm, k), lambda i, j: (i, 0))
    g_spec = lambda br: pl.BlockSpec((tm, tn), lambda i, j: (i, br * (n // tn) + j))
    return pl.pallas_call(
        _merge_kernel,
        out_shape=jax.ShapeDtypeStruct((m, n), BF16),
        grid=(m // tm, n // tn),
        in_specs=[y_spec, y_spec, y_spec, pl.BlockSpec((N_BRANCH, k, tn), lambda i, j: (0, 0, j)),
                  g_spec(0), g_spec(1), g_spec(2)],
        out_specs=pl.BlockSpec((tm, tn), lambda i, j: (i, j)),
        compiler_params=_params("parallel", "arbitrary"),
        name="merge",
    )(ya, yb, yc, w_branch, zm, zm, zm)


def _ple_kernel(h_ref, p_ref, wp_ref, pg_ref, wg_ref, ng_ref, h_out_ref, hn_out_ref):
    h = h_ref[...]
    e = _rms(_dot(p_ref[...].astype(BF16), wp_ref[...]), pg_ref[...])
    h = h + _sigmoid(_dot(h.astype(BF16), wg_ref[...])) * e
    h_out_ref[...] = h
    hn_out_ref[...] = _rms(h, ng_ref[...]).astype(BF16)


def _ple(h, p, w_ple, ple_gain, w_gate, next_gain, *, tm):
    t, d = h.shape
    pd = p.shape[1]
    const = lambda i: (0, 0)
    return pl.pallas_call(
        _ple_kernel,
        out_shape=(jax.ShapeDtypeStruct((t, d), F32), jax.ShapeDtypeStruct((t, d), BF16)),
        grid=(t // tm,),
        in_specs=[pl.BlockSpec((tm, d), lambda i: (i, 0)), pl.BlockSpec((tm, pd), lambda i: (i, 0)),
                  pl.BlockSpec((pd, d), const), pl.BlockSpec((1, d), const),
                  pl.BlockSpec((d, d), const), pl.BlockSpec((1, d), const)],
        out_specs=(pl.BlockSpec((tm, d), lambda i: (i, 0)), pl.BlockSpec((tm, d), lambda i: (i, 0))),
        compiler_params=_params("parallel"),
        name="ple_update",
    )(h, p, w_ple, ple_gain, w_gate, next_gain)


def _in_proj_layout(d_model, branch_width):
    kvw = NSA_KV_HEADS * HEAD_DIM
    heads = branch_width // HEAD_DIM
    splits = (("lru_x", branch_width), ("lru_gate", branch_width), ("nsa_q", branch_width),
              ("nsa_k_cmp", kvw), ("nsa_v_cmp", kvw), ("nsa_k_slc", kvw), ("nsa_v_slc", kvw),
              ("nsa_k_win", kvw), ("nsa_v_win", kvw), ("nsa_bgate", heads * N_BRANCH),
              ("nsa_gate", branch_width), ("fox_q", branch_width), ("fox_k", branch_width),
              ("fox_v", branch_width), ("fox_f", heads), ("fox_gate", branch_width),
              ("merge", N_BRANCH * d_model))
    offs, start = {}, 0
    for name, size in splits:
        offs[name] = (start, size)
        start += size
    return offs


_GROUPS = {
    "za": (("lru_x",), "none", F32),
    "zg": (("lru_gate", "nsa_gate", "fox_gate"), "silu", F32),
    "zq": (("nsa_q", "nsa_k_slc", "nsa_k_win", "fox_q", "fox_k"), "headnorm", BF16),
    "zv": (("nsa_v_slc", "nsa_v_win", "fox_v"), "none", BF16),
    "zc": (("nsa_k_cmp", "nsa_v_cmp", "fox_f", "nsa_bgate"), "none", F32),
    "zm": (("merge",), "sigmoid", F32),
}


def _gather_cols(w, offs, names, pad_to=None):
    parts = [lax.slice_in_dim(w, offs[n][0], offs[n][0] + offs[n][1], axis=w.ndim - 1) for n in names]
    out = jnp.concatenate(parts, axis=-1) if len(parts) > 1 else parts[0]
    if pad_to is not None and out.shape[-1] % pad_to:
        pad = pad_to - out.shape[-1] % pad_to
        out = jnp.pad(out, [(0, 0)] * (out.ndim - 1) + [(0, pad)])
    return out


def _block_diag(w, pack):
    depth, nb, k, _ = w.shape
    per = pack // k
    w = w.reshape(depth, nb // per, per, k, k)
    eye = jnp.eye(per, dtype=w.dtype)
    return jnp.einsum("dgpij,pq->dgpiqj", w, eye).reshape(depth, nb // per, pack, pack)


def _tile(n, target):
    if n <= target:
        return n
    for unit in (MXU_WIDTH, LANES):
        fits = [t for t in range(unit, target + 1, unit) if n % t == 0]
        if fits:
            return fits[-1]
    return LANES


def kernel(x, p, ln_gain, w_in, b_in, conv_w, conv_b, lru_wa, lru_ba, lru_wx, lru_bx, lru_lambda, cmp_w1, cmp_w2,
           cmp_pos, nsa_q_gain, nsa_k_gain, fox_q_gain, fox_k_gain, w_branch, w_out, w_ple, ple_gain, w_ple_gate):
    batch, seq, d_model = x.shape
    depth = w_in.shape[0]
    bw = conv_w.shape[2]
    heads = bw // HEAD_DIM
    t = batch * seq
    assert seq % NSA_TK == 0 and seq % FOX_T == 0 and seq >= WINDOW + NSA_TQ and seq // SEL_BLOCK <= LANES
    offs = _in_proj_layout(d_model, bw)

    wg, bg = {}, {}
    for name, (members, _, _) in _GROUPS.items():
        wg[name] = _gather_cols(w_in, offs, members, pad_to=LANES).astype(BF16)
        bg[name] = _gather_cols(b_in, offs, members, pad_to=LANES)[:, None, :]
    rep = lambda g, n: jnp.tile(g, (1, n))
    q_scale = HEAD_DIM ** -0.5 * LOG2E
    zq_gain = jnp.concatenate([rep(nsa_q_gain * q_scale, heads), rep(nsa_k_gain, 2 * NSA_KV_HEADS),
                               rep(fox_q_gain * q_scale, heads), rep(fox_k_gain, heads)], axis=1)[:, None, :]
    hb = bw // HEAD_DIM
    cols = {"nsa_q": 0, "k_slc": hb, "k_win": hb + NSA_KV_HEADS, "fox_q": hb + 2 * NSA_KV_HEADS,
            "fox_k": 2 * hb + 2 * NSA_KV_HEADS,
            "v_slc": 0, "v_win": NSA_KV_HEADS, "fox_v": 2 * NSA_KV_HEADS,
            "small": 2 * NSA_KV_HEADS, "bgate_lane": heads,
            "nsa_gate": hb, "fox_gate": 2 * hb}
    wa_bd = _block_diag(lru_wa, LRU_PACK).astype(BF16)
    wx_bd = _block_diag(lru_wx, LRU_PACK).astype(BF16)
    w_branch_b = w_branch.astype(BF16)
    w_out_b = w_out.astype(BF16)
    w_ple_b = w_ple.astype(BF16)
    w_gate_b = w_ple_gate.astype(BF16)
    cmp_w1_b = cmp_w1.astype(BF16)
    cmp_w2_b = cmp_w2.astype(BF16)
    zero_bias = jnp.zeros((1, d_model), F32)

    tm = _tile(t, 1024)
    h = x.reshape(t, d_model)
    p2 = p.reshape(depth, t, p.shape[-1])
    hn = _prenorm(h, ln_gain[0][None], _tile(t, 512))
    for l in range(depth):
        z = {}
        for name, (_, epilogue, dtype) in _GROUPS.items():
            n = wg[name].shape[2]
            z[name] = _matmul(hn, wg[name][l], bg[name][l], epilogue=epilogue, out_dtype=dtype, tm=_tile(t, 2048),
                              tn=_tile(n, 1024), extra=zq_gain[l] if epilogue == "headnorm" else None,
                              name="in_proj_" + name)
        ya = _rglru(z["za"], z["zg"], conv_w[l], conv_b[l][None], wa_bd[l], lru_ba[l][None], wx_bd[l],
                    lru_bx[l][None], lru_lambda[l][None], batch=batch, seq=seq, ts=_tile(seq, 512))
        kvc = _compress(z["zc"], cmp_pos[l], cmp_w1_b[l], cmp_w2_b[l], nsa_k_gain[l][None], batch=batch, seq=seq)
        yb = _nsa(z["zq"], z["zv"], kvc, z["zc"], z["zg"], batch=batch, seq=seq, cols=cols)
        kbias = _forget_prefix(z["zc"], batch=batch, seq=seq, heads=heads, col_block=cols["small"])
        yc = _fox(z["zq"], z["zv"], kbias, z["zg"], batch=batch, seq=seq, heads=heads, cols=cols)
        merged = _merge(ya, yb, yc, w_branch_b[l], z["zm"], tm=tm, tn=_tile(d_model, 512))
        h = _matmul(merged, w_out_b[l], zero_bias, epilogue="residual", out_dtype=F32, tm=tm,
                    tn=_tile(d_model, 1024), extra=h, name="out_proj")
        next_gain = ln_gain[(l + 1) % depth][None]
        h, hn = _ple(h, p2[l], w_ple_b[l], ple_gain[l][None], w_gate_b[l], next_gain, tm=_tile(t, 512))
    return h.reshape(batch, seq, d_model)
```

```python
import functools

import jax
import jax.numpy as jnp
from jax import lax
from jax.experimental import pallas as pl
from jax.experimental.pallas import tpu as pltpu

F32 = jnp.float32
BF16 = jnp.bfloat16

HEAD_DIM = 128
NORM_EPS = 1e-6
LRU_BLOCK = 64
CONV_WIDTH = 4
LRU_C = 8.0
NSA_KV_HEADS = 2
NSA_GROUP = 4
CMP_BLOCK = 32
CMP_STRIDE = 16
SEL_BLOCK = 64
SEL_TOPK = 16
WINDOW = 512
N_BRANCH = 3

V7X_VMEM_BYTES = 64 * 1024 * 1024
VMEM_LIMIT = V7X_VMEM_BYTES - 8 * 1024 * 1024
LANES = 128
SUBLANES = 8
MXU_WIDTH = 256
LOG2E = 1.4426950408889634

MASKED = -1e30
FORCED_SCORE = 1e30
INVALID_SCORE = -1.0


def _params(*semantics):
    return pltpu.CompilerParams(dimension_semantics=semantics, vmem_limit_bytes=VMEM_LIMIT)


def _dot(a, b):
    return jnp.dot(a, b, preferred_element_type=F32)


def _dot_t(a, b):
    return lax.dot_general(a, b, (((1,), (1,)), ((), ())), preferred_element_type=F32)


def _split3(x):
    hi = x.astype(BF16)
    r1 = x - hi.astype(F32)
    mid = r1.astype(BF16)
    lo = (r1 - mid.astype(F32)).astype(BF16)
    return hi, mid, lo


def _sigmoid(x):
    return 0.5 * jnp.tanh(0.5 * x) + 0.5


def _softplus(x):
    return jnp.maximum(x, 0.0) + jnp.log(1.0 + jnp.exp(-jnp.abs(x)))


def _rms(x, gain):
    y = x * lax.rsqrt(jnp.mean(x * x, axis=-1, keepdims=True) + NORM_EPS)
    return y * gain


def _prenorm_kernel(h_ref, g_ref, o_ref):
    o_ref[...] = _rms(h_ref[...], g_ref[...]).astype(BF16)


def _prenorm(h, gain, tm):
    t, d = h.shape
    return pl.pallas_call(
        _prenorm_kernel,
        out_shape=jax.ShapeDtypeStruct((t, d), BF16),
        grid=(t // tm,),
        in_specs=[pl.BlockSpec((tm, d), lambda i: (i, 0)), pl.BlockSpec((1, d), lambda i: (0, 0))],
        out_specs=pl.BlockSpec((tm, d), lambda i: (i, 0)),
        compiler_params=_params("parallel"),
        name="prenorm",
    )(h, gain)


def _matmul_kernel(*refs, epilogue):
    if epilogue in ("headnorm", "residual"):
        x_ref, w_ref, b_ref, e_ref, o_ref = refs
    else:
        x_ref, w_ref, b_ref, o_ref = refs
    tn = o_ref.shape[1]
    step = MXU_WIDTH if tn % MXU_WIDTH == 0 else tn
    x = x_ref[...]
    for c0 in range(0, tn, step):
        cols = slice(c0, c0 + step)
        acc = _dot(x, w_ref[:, cols]) + b_ref[:, cols]
        if epilogue == "silu":
            acc = acc * _sigmoid(acc)
        elif epilogue == "sigmoid":
            acc = _sigmoid(acc)
        elif epilogue == "residual":
            acc = acc + e_ref[:, cols]
        if epilogue == "headnorm":
            for h0 in range(c0, c0 + step, HEAD_DIM):
                head = slice(h0, h0 + HEAD_DIM)
                o_ref[:, head] = _rms(acc[:, h0 - c0:h0 - c0 + HEAD_DIM], e_ref[:, head]).astype(o_ref.dtype)
        else:
            o_ref[:, cols] = acc.astype(o_ref.dtype)


def _matmul(x, w, b, *, epilogue, out_dtype, tm, tn, extra=None, name):
    m, k = x.shape
    n = w.shape[1]
    in_specs = [pl.BlockSpec((tm, k), lambda i, j: (i, 0)),
                pl.BlockSpec((k, tn), lambda i, j: (0, j)),
                pl.BlockSpec((1, tn), lambda i, j: (0, j))]
    args = [x, w, b]
    if epilogue == "headnorm":
        in_specs.append(pl.BlockSpec((1, tn), lambda i, j: (0, j)))
        args.append(extra)
    elif epilogue == "residual":
        in_specs.append(pl.BlockSpec((tm, tn), lambda i, j: (i, j)))
        args.append(extra)
    return pl.pallas_call(
        functools.partial(_matmul_kernel, epilogue=epilogue),
        out_shape=jax.ShapeDtypeStruct((m, n), out_dtype),
        grid=(m // tm, n // tn),
        in_specs=in_specs,
        out_specs=pl.BlockSpec((tm, tn), lambda i, j: (i, j)),
        compiler_params=_params("parallel", "arbitrary"),
        name=name,
    )(*args)


LRU_PACK = 256


def _rglru_kernel(u_ref, g_ref, cw_ref, cb_ref, wa_ref, ba_ref, wx_ref, bx_ref, lam_ref, o_ref,
                  ubuf, abuf, bbuf, hbuf, hcar):
    ts, w = u_ref.shape
    s_idx = pl.program_id(1)

    @pl.when(s_idx == 0)
    def _():
        ubuf[0:8, :] = jnp.zeros((8, w), F32)
        hcar[...] = jnp.zeros((1, w), F32)

    ubuf[8:, :] = u_ref[...]
    uc = cb_ref[...] + cw_ref[CONV_WIDTH - 1:CONV_WIDTH, :] * ubuf[8:, :]
    for d in range(1, CONV_WIDTH):
        uc = uc + cw_ref[CONV_WIDTH - 1 - d:CONV_WIDTH - d, :] * ubuf[pl.ds(8 - d, ts), :]
    ubuf[0:8, :] = ubuf[ts:ts + 8, :]

    ucb = uc.astype(BF16)
    neg_c_softplus = -LRU_C * _softplus(-lam_ref[...])
    for c in range(w // LRU_PACK):
        cols = slice(c * LRU_PACK, (c + 1) * LRU_PACK)
        rec = _sigmoid(_dot(ucb[:, cols], wa_ref[c]) + ba_ref[:, cols])
        inp = _sigmoid(_dot(ucb[:, cols], wx_ref[c]) + bx_ref[:, cols])
        log_a = rec * neg_c_softplus[:, cols]
        a = jnp.exp(log_a)
        th = jnp.tanh(log_a)
        one_minus_a2 = -2.0 * th / (1.0 - th)
        abuf[:, cols] = a
        bbuf[:, cols] = jnp.sqrt(one_minus_a2) * (inp * uc[:, cols])

    def step(t, h):
        h = abuf[pl.ds(t, 1), :] * h + bbuf[pl.ds(t, 1), :]
        hbuf[pl.ds(t, 1), :] = h
        return h

    hcar[...] = lax.fori_loop(0, ts, step, hcar[...], unroll=8)
    o_ref[...] = (hbuf[...] * g_ref[...]).astype(o_ref.dtype)


def _rglru(za, zg, conv_w, conv_b, wa_bd, ba, wx_bd, bx, lam, *, batch, seq, ts):
    w = conv_w.shape[1]
    ns = seq // ts
    row = lambda b, s: (b * ns + s, 0)
    const = lambda b, s: (0, 0)
    return pl.pallas_call(
        _rglru_kernel,
        out_shape=jax.ShapeDtypeStruct((batch * seq, w), BF16),
        grid=(batch, ns),
        in_specs=[pl.BlockSpec((ts, w), row), pl.BlockSpec((ts, w), row),
                  pl.BlockSpec((CONV_WIDTH, w), const), pl.BlockSpec((1, w), const),
                  pl.BlockSpec(wa_bd.shape, lambda b, s: (0, 0, 0)), pl.BlockSpec((1, w), const),
                  pl.BlockSpec(wx_bd.shape, lambda b, s: (0, 0, 0)), pl.BlockSpec((1, w), const),
                  pl.BlockSpec((1, w), const)],
        out_specs=pl.BlockSpec((ts, w), row),
        scratch_shapes=[pltpu.VMEM((ts + 8, w), F32), pltpu.VMEM((ts, w), F32), pltpu.VMEM((ts, w), F32),
                        pltpu.VMEM((ts, w), F32), pltpu.VMEM((1, w), F32)],
        compiler_params=_params("parallel", "arbitrary"),
        name="rglru",
    )(za, zg, conv_w, conv_b, wa_bd, ba, wx_bd, bx, lam)


def _compress_kernel(x_ref, pos_ref, w1_ref, w2_ref, gain_ref, o_ref, xpad):
    seq = x_ref.shape[0]
    n_out = seq // CMP_STRIDE
    j = pl.program_id(1)
    xpad[0:seq, :] = x_ref[...]
    xpad[seq:, :] = jnp.zeros((CMP_BLOCK, HEAD_DIM), F32)
    hidden = jnp.zeros((n_out, w1_ref.shape[2]), F32)
    for r in range(CMP_BLOCK):
        rows = xpad[pl.ds(r, n_out, stride=CMP_STRIDE), :] + pos_ref[0, r:r + 1, :]
        hidden = hidden + _dot(rows.astype(BF16), w1_ref[0, r * HEAD_DIM:(r + 1) * HEAD_DIM, :])
    hidden = hidden * _sigmoid(hidden)
    out = _dot(hidden.astype(BF16), w2_ref[0])
    out = jnp.where(j < NSA_KV_HEADS, _rms(out, gain_ref[...]), out)
    row = lax.broadcasted_iota(jnp.int32, out.shape, 0)
    o_ref[0, 0] = jnp.where(row < n_out - 1, out, 0.0).astype(o_ref.dtype)


def _compress(zc, cmp_pos, cmp_w1, cmp_w2, k_gain, *, batch, seq):
    n_out = seq // CMP_STRIDE
    hid = cmp_w1.shape[2]
    return pl.pallas_call(
        _compress_kernel,
        out_shape=jax.ShapeDtypeStruct((batch, 2 * NSA_KV_HEADS, n_out, HEAD_DIM), BF16),
        grid=(batch, 2 * NSA_KV_HEADS),
        in_specs=[pl.BlockSpec((seq, HEAD_DIM), lambda b, j: (b, j)),
                  pl.BlockSpec((1, CMP_BLOCK, HEAD_DIM), lambda b, j: (j // NSA_KV_HEADS, 0, 0)),
                  pl.BlockSpec((1, CMP_BLOCK * HEAD_DIM, hid), lambda b, j: (j // NSA_KV_HEADS, 0, 0)),
                  pl.BlockSpec((1, hid, HEAD_DIM), lambda b, j: (j // NSA_KV_HEADS, 0, 0)),
                  pl.BlockSpec((1, HEAD_DIM), lambda b, j: (0, 0))],
        out_specs=pl.BlockSpec((1, 1, n_out, HEAD_DIM), lambda b, j: (b, j, 0, 0)),
        scratch_shapes=[pltpu.VMEM((seq + CMP_BLOCK, HEAD_DIM), F32)],
        compiler_params=_params("parallel", "arbitrary"),
        name="compress",
    )(zc, cmp_pos, cmp_w1, cmp_w2, k_gain)


FOX_BIAS_TERMS = 3


def _forget_prefix_kernel(s_ref, o_ref, *, heads):
    seq = s_ref.shape[0]
    f = s_ref[...]
    log_f = jnp.minimum(f, 0.0) - jnp.log(1.0 + jnp.exp(-jnp.abs(f)))
    r = lax.broadcasted_iota(jnp.int32, (LANES, LANES), 0)
    c = lax.broadcasted_iota(jnp.int32, (LANES, LANES), 1)
    lower = jnp.where(c <= r, 1.0, 0.0).astype(BF16)
    pr = lax.broadcasted_iota(jnp.int32, (LANES, heads * LANES), 0)
    pc = lax.broadcasted_iota(jnp.int32, (LANES, heads * LANES), 1)
    place = [jnp.where((pc == pr * LANES + j) & (pr < heads), 1.0, 0.0).astype(BF16) for j in range(FOX_BIAS_TERMS)]
    carry = jnp.zeros((1, LANES), F32)
    for k in range(seq // LANES):
        rows = slice(k * LANES, (k + 1) * LANES)
        hi, mid, lo = _split3(log_f[rows])
        chunk = (_dot(lower, lo) + _dot(lower, mid)) + _dot(lower, hi) + carry
        carry = chunk[LANES - 1:LANES, :]
        terms = _split3(chunk * (-LOG2E))
        out = _dot(terms[0], place[0])
        for j in range(1, FOX_BIAS_TERMS):
            out = out + _dot(terms[j], place[j])
        o_ref[rows, :] = out.astype(o_ref.dtype)


def _forget_prefix(zc, *, batch, seq, heads, col_block):
    return pl.pallas_call(
        functools.partial(_forget_prefix_kernel, heads=heads),
        out_shape=jax.ShapeDtypeStruct((batch * seq, heads * LANES), BF16),
        grid=(batch,),
        in_specs=[pl.BlockSpec((seq, LANES), lambda b: (b, col_block))],
        out_specs=pl.BlockSpec((seq, heads * LANES), lambda b: (b, 0)),
        compiler_params=_params("parallel"),
        name="forget_prefix",
    )(zc)


NSA_TQ = 256
NSA_TK = 512
SEL_SHIFT = SEL_BLOCK.bit_length() - 1


def _biased_exp2(s, bias):
    rows, k = s.shape
    y = (s.reshape(NSA_GROUP, rows // NSA_GROUP, k) + bias[None]).reshape(rows, k)
    e = jnp.exp2(y - jnp.max(y, axis=-1, keepdims=True))
    return e, jnp.sum(e, axis=-1, keepdims=True)


def _top_rows_bias(score, n_top):
    n, t = score.shape
    n_groups = n // SUBLANES
    groups = [score[r * SUBLANES:(r + 1) * SUBLANES] for r in range(n_groups)]
    ranks = [jnp.zeros((SUBLANES, t), F32) for _ in range(n_groups)]
    local = lax.broadcasted_iota(jnp.int32, (SUBLANES, t), 0)
    for i in range(n):
        gi, li = divmod(i, SUBLANES)
        s_i = groups[gi][li:li + 1, :]
        for r in range(n_groups):
            if r > gi:
                ahead = s_i >= groups[r]
            elif r < gi:
                ahead = s_i > groups[r]
            else:
                ahead = (s_i > groups[r]) | ((s_i == groups[r]) & (local > li))
            ranks[r] = ranks[r] + jnp.where(ahead, 1.0, 0.0)
    return jnp.where(jnp.concatenate(ranks, axis=0) < n_top, 0.0, MASKED)


def _nsa_kernel(q_ref, kvc_ref, ks0_ref, ks1_ref, kw0_ref, kw1_ref, vs0_ref, vs1_ref, vw0_ref, vw1_ref,
                small_ref, gate_ref, o_ref, ks_aug, *, n_top, bgate_lane):
    tq = NSA_TQ
    rows = NSA_GROUP * tq
    seq = ks0_ref.shape[0]
    n_cmp_pad = kvc_ref.shape[2]
    n_sel = seq // SEL_BLOCK
    t0 = pl.multiple_of(pl.program_id(1) * tq, tq)
    branch_gate = _sigmoid(small_ref[...])

    @pl.when(pl.program_id(1) == 0)
    def _():
        blk_of_key = lax.broadcasted_iota(jnp.int32, (seq, LANES), 0) >> SEL_SHIFT
        one_hot = jnp.where(blk_of_key == lax.broadcasted_iota(jnp.int32, (seq, LANES), 1), 1.0, 0.0).astype(BF16)
        for g, ks_ref in enumerate((ks0_ref, ks1_ref)):
            ks_aug[g, :, 0:HEAD_DIM] = ks_ref[...]
            ks_aug[g, :, HEAD_DIM:] = one_hot

    tok_col = t0 + lax.broadcasted_iota(jnp.int32, (tq, 1), 0)
    tok_rows = jnp.broadcast_to(tok_col[None], (NSA_GROUP, tq, 1)).reshape(rows, 1)

    for g, (ks_ref, kw_ref, vs_ref, vw_ref) in enumerate(
            ((ks0_ref, kw0_ref, vs0_ref, vw0_ref), (ks1_ref, kw1_ref, vs1_ref, vw1_ref))):
        q4 = jnp.concatenate(
            [q_ref[:, (g * NSA_GROUP + h) * HEAD_DIM:(g * NSA_GROUP + h + 1) * HEAD_DIM] for h in range(NSA_GROUP)],
            axis=0)

        kc = kvc_ref[0, g]
        vc = kvc_ref[0, NSA_KV_HEADS + g]
        cmp_end = lax.broadcasted_iota(jnp.int32, (1, n_cmp_pad), 1) * CMP_STRIDE + (CMP_BLOCK - 1)
        e_c, sum_c = _biased_exp2(_dot_t(q4, kc), jnp.where(cmp_end <= tok_col, 0.0, MASKED))
        p_c = e_c * jnp.where(tok_rows >= CMP_BLOCK - 1, 1.0 / sum_c, 0.0)
        o_c = _dot(p_c.astype(BF16), vc)

        p_sum = p_c[0:tq]
        for h in range(1, NSA_GROUP):
            p_sum = p_sum + p_c[h * tq:(h + 1) * tq]
        jj = lax.broadcasted_iota(jnp.int32, (n_sel, n_cmp_pad), 0)
        cc = lax.broadcasted_iota(jnp.int32, (n_sel, n_cmp_pad), 1)
        overlap_t = jnp.where((cc * CMP_STRIDE < (jj + 1) * SEL_BLOCK) & (cc * CMP_STRIDE + CMP_BLOCK > jj * SEL_BLOCK)
                              & (cc < n_cmp_pad - 1), 1.0, 0.0).astype(BF16)
        hi, mid, lo = _split3(p_sum)
        imp_t = (_dot_t(overlap_t, lo) + _dot_t(overlap_t, mid)) + _dot_t(overlap_t, hi)
        blk = lax.broadcasted_iota(jnp.int32, (n_sel, tq), 0)
        cur = (t0 + lax.broadcasted_iota(jnp.int32, (n_sel, tq), 1)) >> SEL_SHIFT
        forced = (blk == 0) | (blk == cur) | (blk == cur - 1)
        score = jnp.where(forced, FORCED_SCORE, jnp.where(blk <= cur, imp_t, INVALID_SCORE))
        sel_bias = _top_rows_bias(score, n_top)
        sel_bias = jnp.concatenate([sel_bias, jnp.zeros((LANES - n_sel, tq), F32)], axis=0).T.astype(BF16)
        q_aug = jnp.concatenate([q4, jnp.concatenate([sel_bias] * NSA_GROUP, axis=0)], axis=1)

        def sel_tile(kt, carry, causal):
            m, l, acc = carry
            k0 = pl.multiple_of(kt * NSA_TK, NSA_TK)
            y = _dot_t(q_aug, ks_aug[g, pl.ds(k0, NSA_TK), :])
            if causal:
                kpos = k0 + lax.broadcasted_iota(jnp.int32, (1, NSA_TK), 1)
                y = (y.reshape(NSA_GROUP, tq, NSA_TK) + jnp.where(kpos <= tok_col, 0.0, MASKED)[None]).reshape(
                    rows, NSA_TK)
            m_new = jnp.maximum(m, jnp.max(y, axis=-1, keepdims=True))
            alpha = jnp.exp2(m - m_new)
            e = jnp.exp2(y - m_new)
            l = alpha * l + jnp.sum(e, axis=-1, keepdims=True)
            acc = alpha * acc + _dot(e.astype(BF16), vs_ref[pl.ds(k0, NSA_TK), :])
            return m_new, l, acc

        kd = t0 // NSA_TK
        init = (jnp.full((rows, 1), 3.0 * MASKED, F32), jnp.zeros((rows, 1), F32),
                jnp.zeros((rows, HEAD_DIM), F32))

        def sel_pair(j, carry):
            return sel_tile(2 * j + 1, sel_tile(2 * j, carry, False), False)

        carry = lax.fori_loop(0, kd // 2, sel_pair, init)
        carry = lax.fori_loop(kd - kd % 2, kd, functools.partial(sel_tile, causal=False), carry)
        _, l, acc = sel_tile(kd, carry, True)
        o_s = acc * (1.0 / l)

        w0 = pl.multiple_of(jnp.maximum(t0 - WINDOW, 0), tq)
        wpos = w0 + lax.broadcasted_iota(jnp.int32, (1, WINDOW + tq), 1)
        e_w, sum_w = _biased_exp2(_dot_t(q4, kw_ref[pl.ds(w0, WINDOW + tq), :]),
                                  jnp.where((wpos <= tok_col) & (wpos > tok_col - WINDOW), 0.0, MASKED))
        o_w = _dot(e_w.astype(BF16), vw_ref[pl.ds(w0, WINDOW + tq), :]) * (1.0 / sum_w)

        for h in range(NSA_GROUP):
            head = g * NSA_GROUP + h
            r = slice(h * tq, (h + 1) * tq)
            lane = bgate_lane + head * N_BRANCH
            mix = (branch_gate[:, lane:lane + 1] * o_c[r] + branch_gate[:, lane + 1:lane + 2] * o_s[r]
                   + branch_gate[:, lane + 2:lane + 3] * o_w[r])
            cols = slice(head * HEAD_DIM, (head + 1) * HEAD_DIM)
            o_ref[:, cols] = (mix * gate_ref[:, cols]).astype(o_ref.dtype)


def _nsa(zq, zv, kvc, zc, zg, *, batch, seq, cols):
    nq = seq // NSA_TQ
    width = NSA_KV_HEADS * NSA_GROUP * HEAD_DIM
    n_top = min(SEL_TOPK, seq // SEL_BLOCK)
    row = lambda b, i: b * nq + i
    kv_spec = lambda c: pl.BlockSpec((seq, HEAD_DIM), lambda b, i: (b, c))
    return pl.pallas_call(
        functools.partial(_nsa_kernel, n_top=n_top, bgate_lane=cols["bgate_lane"]),
        out_shape=jax.ShapeDtypeStruct((batch * seq, width), BF16),
        grid=(batch, nq),
        in_specs=[pl.BlockSpec((NSA_TQ, width), lambda b, i: (row(b, i), cols["nsa_q"] * HEAD_DIM // width)),
                  pl.BlockSpec((1,) + kvc.shape[1:], lambda b, i: (b, 0, 0, 0)),
                  kv_spec(cols["k_slc"]), kv_spec(cols["k_slc"] + 1),
                  kv_spec(cols["k_win"]), kv_spec(cols["k_win"] + 1),
                  kv_spec(cols["v_slc"]), kv_spec(cols["v_slc"] + 1),
                  kv_spec(cols["v_win"]), kv_spec(cols["v_win"] + 1),
                  pl.BlockSpec((NSA_TQ, LANES), lambda b, i: (row(b, i), cols["small"])),
                  pl.BlockSpec((NSA_TQ, width), lambda b, i: (row(b, i), cols["nsa_gate"] * HEAD_DIM // width))],
        out_specs=pl.BlockSpec((NSA_TQ, width), lambda b, i: (row(b, i), 0)),
        scratch_shapes=[pltpu.VMEM((NSA_KV_HEADS, seq, HEAD_DIM + LANES), BF16)],
        compiler_params=_params("parallel", "arbitrary"),
        name="sparse_attention",
    )(zq, kvc, zq, zq, zq, zq, zv, zv, zv, zv, zc, zg)


FOX_T = 1024
FOX_TK = 512


def _fox_kernel(q_ref, k_ref, kb_ref, v_ref, gate_ref, o_ref, k_aug):
    t, tk = FOX_T, FOX_TK
    i = pl.program_id(2)
    q0 = i * t

    @pl.when(i == 0)
    def _():
        k_aug[:, 0:HEAD_DIM] = k_ref[...]
        k_aug[:, HEAD_DIM:] = kb_ref[...]

    lane = lax.broadcasted_iota(jnp.int32, (t, LANES), 1)
    q_aug = jnp.concatenate([q_ref[...], jnp.where(lane < FOX_BIAS_TERMS, 1.0, 0.0).astype(BF16)], axis=1)

    def tile(kt, carry, causal):
        m, l, acc = carry
        k0 = pl.multiple_of(kt * tk, tk)
        y = _dot_t(q_aug, k_aug[pl.ds(k0, tk), :])
        if causal:
            qpos = q0 + lax.broadcasted_iota(jnp.int32, (t, tk), 0)
            kpos = k0 + lax.broadcasted_iota(jnp.int32, (t, tk), 1)
            y = jnp.where(kpos <= qpos, y, MASKED)
        m_new = jnp.maximum(m, jnp.max(y, axis=-1, keepdims=True))
        alpha = jnp.exp2(m - m_new)
        e = jnp.exp2(y - m_new)
        l = alpha * l + jnp.sum(e, axis=-1, keepdims=True)
        acc = alpha * acc + _dot(e.astype(BF16), v_ref[pl.ds(k0, tk), :])
        return m_new, l, acc

    per_q = t // tk
    n_full = i * per_q

    def tile_group(j, carry):
        for d in range(per_q):
            carry = tile(j * per_q + d, carry, False)
        return carry

    carry = (jnp.full((t, 1), MASKED, F32), jnp.zeros((t, 1), F32), jnp.zeros((t, HEAD_DIM), F32))
    carry = lax.fori_loop(0, i, tile_group, carry)
    for d in range(per_q):
        carry = tile(n_full + d, carry, True)
    _, l, acc = carry
    o_ref[...] = (acc * (1.0 / l) * gate_ref[...]).astype(o_ref.dtype)


def _fox(zq, zv, kbias, zg, *, batch, seq, heads, cols):
    nq = seq // FOX_T
    row = lambda b, h, i: b * nq + i
    return pl.pallas_call(
        _fox_kernel,
        out_shape=jax.ShapeDtypeStruct((batch * seq, heads * HEAD_DIM), BF16),
        grid=(batch, heads, nq),
        in_specs=[pl.BlockSpec((FOX_T, HEAD_DIM), lambda b, h, i: (row(b, h, i), cols["fox_q"] + h)),
                  pl.BlockSpec((seq, HEAD_DIM), lambda b, h, i: (b, cols["fox_k"] + h)),
                  pl.BlockSpec((seq, LANES), lambda b, h, i: (b, h)),
                  pl.BlockSpec((seq, HEAD_DIM), lambda b, h, i: (b, cols["fox_v"] + h)),
                  pl.BlockSpec((FOX_T, HEAD_DIM), lambda b, h, i: (row(b, h, i), cols["fox_gate"] + h))],
        out_specs=pl.BlockSpec((FOX_T, HEAD_DIM), lambda b, h, i: (row(b, h, i), h)),
        scratch_shapes=[pltpu.VMEM((seq, HEAD_DIM + LANES), BF16)],
        compiler_params=_params("parallel", "parallel", "arbitrary"),
        name="forgetting_attention",
    )(zq, zq, kbias, zv, zg)


def _merge_kernel(ya_ref, yb_ref, yc_ref, w_ref, ga_ref, gb_ref, gc_ref, o_ref):
    tn = o_ref.shape[1]
    step = MXU_WIDTH if tn % MXU_WIDTH == 0 else tn
    ya, yb, yc = ya_ref[...], yb_ref[...], yc_ref[...]
    for c0 in range(0, tn, step):
        cols = slice(c0, c0 + step)
        merged = ga_ref[:, cols] * _dot(ya, w_ref[0, :, cols])
        merged = merged + gb_ref[:, cols] * _dot(yb, w_ref[1, :, cols])
        merged = merged + gc_ref[:, cols] * _dot(yc, w_ref[2, :, cols])
        o_ref[:, cols] = merged.astype(o_ref.dtype)


def _merge(ya, yb, yc, w_branch, zm, *, tm, tn):
    m, k = ya.shape
    n = w_branch.shape[2]
    y_spec = pl.BlockSpec((tm, k), lambda i, j: (i, 0))
    g_spec = lambda br: pl.BlockSpec((tm, tn), lambda i, j: (i, br * (n // tn) + j))
    return pl.pallas_call(
        _merge_kernel,
        out_shape=jax.ShapeDtypeStruct((m, n), BF16),
        grid=(m // tm, n // tn),
        in_specs=[y_spec, y_spec, y_spec, pl.BlockSpec((N_BRANCH, k, tn), lambda i, j: (0, 0, j)),
                  g_spec(0), g_spec(1), g_spec(2)],
        out_specs=pl.BlockSpec((tm, tn), lambda i, j: (i, j)),
        compiler_params=_params("parallel", "arbitrary"),
        name="merge",
    )(ya, yb, yc, w_branch, zm, zm, zm)


def _ple_kernel(h_ref, p_ref, wp_ref, pg_ref, wg_ref, ng_ref, h_out_ref, hn_out_ref):
    h = h_ref[...]
    e = _rms(_dot(p_ref[...].astype(BF16), wp_ref[...]), pg_ref[...])
    h = h + _sigmoid(_dot(h.astype(BF16), wg_ref[...])) * e
    h_out_ref[...] = h
    hn_out_ref[...] = _rms(h, ng_ref[...]).astype(BF16)


def _ple(h, p, w_ple, ple_gain, w_gate, next_gain, *, tm):
    t, d = h.shape
    pd = p.shape[1]
    const = lambda i: (0, 0)
    return pl.pallas_call(
        _ple_kernel,
        out_shape=(jax.ShapeDtypeStruct((t, d), F32), jax.ShapeDtypeStruct((t, d), BF16)),
        grid=(t // tm,),
        in_specs=[pl.BlockSpec((tm, d), lambda i: (i, 0)), pl.BlockSpec((tm, pd), lambda i: (i, 0)),
                  pl.BlockSpec((pd, d), const), pl.BlockSpec((1, d), const),
                  pl.BlockSpec((d, d), const), pl.BlockSpec((1, d), const)],
        out_specs=(pl.BlockSpec((tm, d), lambda i: (i, 0)), pl.BlockSpec((tm, d), lambda i: (i, 0))),
        compiler_params=_params("parallel"),
        name="ple_update",
    )(h, p, w_ple, ple_gain, w_gate, next_gain)


def _in_proj_layout(d_model, branch_width):
    kvw = NSA_KV_HEADS * HEAD_DIM
    heads = branch_width // HEAD_DIM
    splits = (("lru_x", branch_width), ("lru_gate", branch_width), ("nsa_q", branch_width),
              ("nsa_k_cmp", kvw), ("nsa_v_cmp", kvw), ("nsa_k_slc", kvw), ("nsa_v_slc", kvw),
              ("nsa_k_win", kvw), ("nsa_v_win", kvw), ("nsa_bgate", heads * N_BRANCH),
              ("nsa_gate", branch_width), ("fox_q", branch_width), ("fox_k", branch_width),
              ("fox_v", branch_width), ("fox_f", heads), ("fox_gate", branch_width),
              ("merge", N_BRANCH * d_model))
    offs, start = {}, 0
    for name, size in splits:
        offs[name] = (start, size)
        start += size
    return offs


_GROUPS = {
    "za": (("lru_x",), "none", F32),
    "zg": (("lru_gate", "nsa_gate", "fox_gate"), "silu", BF16),
    "zq": (("nsa_q", "nsa_k_slc", "nsa_k_win", "fox_q", "fox_k"), "headnorm", BF16),
    "zv": (("nsa_v_slc", "nsa_v_win", "fox_v"), "none", BF16),
    "zc": (("nsa_k_cmp", "nsa_v_cmp", "fox_f", "nsa_bgate"), "none", F32),
    "zm": (("merge",), "sigmoid", BF16),
}


def _gather_cols(w, offs, names, pad_to=None):
    parts = [lax.slice_in_dim(w, offs[n][0], offs[n][0] + offs[n][1], axis=w.ndim - 1) for n in names]
    out = jnp.concatenate(parts, axis=-1) if len(parts) > 1 else parts[0]
    if pad_to is not None and out.shape[-1] % pad_to:
        pad = pad_to - out.shape[-1] % pad_to
        out = jnp.pad(out, [(0, 0)] * (out.ndim - 1) + [(0, pad)])
    return out


def _block_diag(w, pack):
    depth, nb, k, _ = w.shape
    per = pack // k
    w = w.reshape(depth, nb // per, per, k, k)
    eye = jnp.eye(per, dtype=w.dtype)
    return jnp.einsum("dgpij,pq->dgpiqj", w, eye).reshape(depth, nb // per, pack, pack)


def _tile(n, target):
    if n <= target:
        return n
    for unit in (MXU_WIDTH, LANES):
        fits = [t for t in range(unit, target + 1, unit) if n % t == 0]
        if fits:
            return fits[-1]
    return LANES


def kernel(x, p, ln_gain, w_in, b_in, conv_w, conv_b, lru_wa, lru_ba, lru_wx, lru_bx, lru_lambda, cmp_w1, cmp_w2,
           cmp_pos, nsa_q_gain, nsa_k_gain, fox_q_gain, fox_k_gain, w_branch, w_out, w_ple, ple_gain, w_ple_gate):
    batch, seq, d_model = x.shape
    depth = w_in.shape[0]
    bw = conv_w.shape[2]
    heads = bw // HEAD_DIM
    t = batch * seq
    assert seq % NSA_TK == 0 and seq % FOX_T == 0 and seq >= WINDOW + NSA_TQ and seq // SEL_BLOCK <= LANES
    offs = _in_proj_layout(d_model, bw)

    wg, bg = {}, {}
    for name, (members, _, _) in _GROUPS.items():
        wg[name] = _gather_cols(w_in, offs, members, pad_to=LANES).astype(BF16)
        bg[name] = _gather_cols(b_in, offs, members, pad_to=LANES)[:, None, :]
    rep = lambda g, n: jnp.tile(g, (1, n))
    q_scale = HEAD_DIM ** -0.5 * LOG2E
    zq_gain = jnp.concatenate([rep(nsa_q_gain * q_scale, heads), rep(nsa_k_gain, 2 * NSA_KV_HEADS),
                               rep(fox_q_gain * q_scale, heads), rep(fox_k_gain, heads)], axis=1)[:, None, :]
    hb = bw // HEAD_DIM
    cols = {"nsa_q": 0, "k_slc": hb, "k_win": hb + NSA_KV_HEADS, "fox_q": hb + 2 * NSA_KV_HEADS,
            "fox_k": 2 * hb + 2 * NSA_KV_HEADS,
            "v_slc": 0, "v_win": NSA_KV_HEADS, "fox_v": 2 * NSA_KV_HEADS,
            "small": 2 * NSA_KV_HEADS, "bgate_lane": heads,
            "nsa_gate": hb, "fox_gate": 2 * hb}
    wa_bd = _block_diag(lru_wa, LRU_PACK).astype(BF16)
    wx_bd = _block_diag(lru_wx, LRU_PACK).astype(BF16)
    w_branch_b = w_branch.astype(BF16)
    w_out_b = w_out.astype(BF16)
    w_ple_b = w_ple.astype(BF16)
    w_gate_b = w_ple_gate.astype(BF16)
    cmp_w1_b = cmp_w1.astype(BF16)
    cmp_w2_b = cmp_w2.astype(BF16)
    zero_bias = jnp.zeros((1, d_model), F32)

    tm = _tile(t, 1024)
    h = x.reshape(t, d_model)
    p2 = p.reshape(depth, t, p.shape[-1])
    hn = _prenorm(h, ln_gain[0][None], _tile(t, 512))
    for l in range(depth):
        z = {}
        for name, (_, epilogue, dtype) in _GROUPS.items():
            n = wg[name].shape[2]
            z[name] = _matmul(hn, wg[name][l], bg[name][l], epilogue=epilogue, out_dtype=dtype, tm=_tile(t, 2048),
                              tn=_tile(n, 1024), extra=zq_gain[l] if epilogue == "headnorm" else None,
                              name="in_proj_" + name)
        ya = _rglru(z["za"], z["zg"], conv_w[l], conv_b[l][None], wa_bd[l], lru_ba[l][None], wx_bd[l],
                    lru_bx[l][None], lru_lambda[l][None], batch=batch, seq=seq, ts=_tile(seq, 512))
        kvc = _compress(z["zc"], cmp_pos[l], cmp_w1_b[l], cmp_w2_b[l], nsa_k_gain[l][None], batch=batch, seq=seq)
        yb = _nsa(z["zq"], z["zv"], kvc, z["zc"], z["zg"], batch=batch, seq=seq, cols=cols)
        kbias = _forget_prefix(z["zc"], batch=batch, seq=seq, heads=heads, col_block=cols["small"])
        yc = _fox(z["zq"], z["zv"], kbias, z["zg"], batch=batch, seq=seq, heads=heads, cols=cols)
        merged = _merge(ya, yb, yc, w_branch_b[l], z["zm"], tm=tm, tn=_tile(d_model, 512))
        h = _matmul(merged, w_out_b[l], zero_bias, epilogue="residual", out_dtype=F32, tm=tm,
                    tn=_tile(d_model, 1024), extra=h, name="out_proj")
        next_gain = ln_gain[(l + 1) % depth][None]
        h, hn = _ple(h, p2[l], w_ple_b[l], ple_gain[l][None], w_gate_b[l], next_gain, tm=_tile(t, 512))
    return h.reshape(batch, seq, d_model)
```

```python
import functools

import jax
import jax.numpy as jnp
from jax import lax
from jax.experimental import pallas as pl
from jax.experimental.pallas import tpu as pltpu

F32 = jnp.float32
BF16 = jnp.bfloat16

HEAD_DIM = 128
NORM_EPS = 1e-6
LRU_BLOCK = 64
CONV_WIDTH = 4
LRU_C = 8.0
NSA_KV_HEADS = 2
NSA_GROUP = 4
CMP_BLOCK = 32
CMP_STRIDE = 16
SEL_BLOCK = 64
SEL_TOPK = 16
WINDOW = 512
N_BRANCH = 3

V7X_VMEM_BYTES = 64 * 1024 * 1024
VMEM_LIMIT = V7X_VMEM_BYTES - 8 * 1024 * 1024
LANES = 128
SUBLANES = 8
MXU_WIDTH = 256
LOG2E = 1.4426950408889634

MASKED = -1e30
FORCED_SCORE = 1e30
INVALID_SCORE = -1.0


def _params(*semantics):
    return pltpu.CompilerParams(dimension_semantics=semantics, vmem_limit_bytes=VMEM_LIMIT)


def _dot(a, b):
    return jnp.dot(a, b, preferred_element_type=F32)


def _dot_t(a, b):
    return lax.dot_general(a, b, (((1,), (1,)), ((), ())), preferred_element_type=F32)


def _split3(x):
    hi = x.astype(BF16)
    r1 = x - hi.astype(F32)
    mid = r1.astype(BF16)
    lo = (r1 - mid.astype(F32)).astype(BF16)
    return hi, mid, lo


def _sigmoid(x):
    return 0.5 * jnp.tanh(0.5 * x) + 0.5


def _softplus(x):
    return jnp.maximum(x, 0.0) + jnp.log(1.0 + jnp.exp(-jnp.abs(x)))


def _rms(x, gain):
    y = x * lax.rsqrt(jnp.mean(x * x, axis=-1, keepdims=True) + NORM_EPS)
    return y * gain


def _prenorm_kernel(h_ref, g_ref, o_ref):
    o_ref[...] = _rms(h_ref[...], g_ref[...]).astype(BF16)


def _prenorm(h, gain, tm):
    t, d = h.shape
    return pl.pallas_call(
        _prenorm_kernel,
        out_shape=jax.ShapeDtypeStruct((t, d), BF16),
        grid=(t // tm,),
        in_specs=[pl.BlockSpec((tm, d), lambda i: (i, 0)), pl.BlockSpec((1, d), lambda i: (0, 0))],
        out_specs=pl.BlockSpec((tm, d), lambda i: (i, 0)),
        compiler_params=_params("parallel"),
        name="prenorm",
    )(h, gain)


def _matmul_kernel(*refs, epilogue):
    if epilogue in ("headnorm", "residual"):
        x_ref, w_ref, b_ref, e_ref, o_ref = refs
    else:
        x_ref, w_ref, b_ref, o_ref = refs
    tn = o_ref.shape[1]
    step = MXU_WIDTH if tn % MXU_WIDTH == 0 else tn
    x = x_ref[...]
    for c0 in range(0, tn, step):
        cols = slice(c0, c0 + step)
        acc = _dot(x, w_ref[:, cols]) + b_ref[:, cols]
        if epilogue == "silu":
            acc = acc * _sigmoid(acc)
        elif epilogue == "sigmoid":
            acc = _sigmoid(acc)
        elif epilogue == "residual":
            acc = acc + e_ref[:, cols]
        if epilogue == "headnorm":
            for h0 in range(c0, c0 + step, HEAD_DIM):
                head = slice(h0, h0 + HEAD_DIM)
                o_ref[:, head] = _rms(acc[:, h0 - c0:h0 - c0 + HEAD_DIM], e_ref[:, head]).astype(o_ref.dtype)
        else:
            o_ref[:, cols] = acc.astype(o_ref.dtype)


def _matmul(x, w, b, *, epilogue, out_dtype, tm, tn, extra=None, name):
    m, k = x.shape
    n = w.shape[1]
    in_specs = [pl.BlockSpec((tm, k), lambda i, j: (i, 0)),
                pl.BlockSpec((k, tn), lambda i, j: (0, j)),
                pl.BlockSpec((1, tn), lambda i, j: (0, j))]
    args = [x, w, b]
    if epilogue == "headnorm":
        in_specs.append(pl.BlockSpec((1, tn), lambda i, j: (0, j)))
        args.append(extra)
    elif epilogue == "residual":
        in_specs.append(pl.BlockSpec((tm, tn), lambda i, j: (i, j)))
        args.append(extra)
    return pl.pallas_call(
        functools.partial(_matmul_kernel, epilogue=epilogue),
        out_shape=jax.ShapeDtypeStruct((m, n), out_dtype),
        grid=(m // tm, n // tn),
        in_specs=in_specs,
        out_specs=pl.BlockSpec((tm, tn), lambda i, j: (i, j)),
        compiler_params=_params("parallel", "arbitrary"),
        name=name,
    )(*args)


LRU_PACK = 256


def _rglru_kernel(u_ref, g_ref, cw_ref, cb_ref, wa_ref, ba_ref, wx_ref, bx_ref, lam_ref, o_ref,
                  ubuf, abuf, bbuf, hbuf, hcar):
    ts, w = u_ref.shape
    s_idx = pl.program_id(1)

    @pl.when(s_idx == 0)
    def _():
        ubuf[0:8, :] = jnp.zeros((8, w), F32)
        hcar[...] = jnp.zeros((1, w), F32)

    ubuf[8:, :] = u_ref[...]
    uc = cb_ref[...] + cw_ref[CONV_WIDTH - 1:CONV_WIDTH, :] * ubuf[8:, :]
    for d in range(1, CONV_WIDTH):
        uc = uc + cw_ref[CONV_WIDTH - 1 - d:CONV_WIDTH - d, :] * ubuf[pl.ds(8 - d, ts), :]
    ubuf[0:8, :] = ubuf[ts:ts + 8, :]

    ucb = uc.astype(BF16)
    neg_c_softplus = -LRU_C * _softplus(-lam_ref[...])
    for c in range(w // LRU_PACK):
        cols = slice(c * LRU_PACK, (c + 1) * LRU_PACK)
        rec = _sigmoid(_dot(ucb[:, cols], wa_ref[c]) + ba_ref[:, cols])
        inp = _sigmoid(_dot(ucb[:, cols], wx_ref[c]) + bx_ref[:, cols])
        log_a = rec * neg_c_softplus[:, cols]
        a = jnp.exp(log_a)
        th = jnp.tanh(log_a)
        one_minus_a2 = -2.0 * th / (1.0 - th)
        abuf[:, cols] = a
        bbuf[:, cols] = jnp.sqrt(one_minus_a2) * (inp * uc[:, cols])

    def step(t, h):
        h = abuf[pl.ds(t, 1), :] * h + bbuf[pl.ds(t, 1), :]
        hbuf[pl.ds(t, 1), :] = h
        return h

    hcar[...] = lax.fori_loop(0, ts, step, hcar[...], unroll=8)
    o_ref[...] = (hbuf[...] * g_ref[...]).astype(o_ref.dtype)


def _rglru(za, zg, conv_w, conv_b, wa_bd, ba, wx_bd, bx, lam, *, batch, seq, ts):
    w = conv_w.shape[1]
    ns = seq // ts
    row = lambda b, s: (b * ns + s, 0)
    const = lambda b, s: (0, 0)
    return pl.pallas_call(
        _rglru_kernel,
        out_shape=jax.ShapeDtypeStruct((batch * seq, w), BF16),
        grid=(batch, ns),
        in_specs=[pl.BlockSpec((ts, w), row), pl.BlockSpec((ts, w), row),
                  pl.BlockSpec((CONV_WIDTH, w), const), pl.BlockSpec((1, w), const),
                  pl.BlockSpec(wa_bd.shape, lambda b, s: (0, 0, 0)), pl.BlockSpec((1, w), const),
                  pl.BlockSpec(wx_bd.shape, lambda b, s: (0, 0, 0)), pl.BlockSpec((1, w), const),
                  pl.BlockSpec((1, w), const)],
        out_specs=pl.BlockSpec((ts, w), row),
        scratch_shapes=[pltpu.VMEM((ts + 8, w), F32), pltpu.VMEM((ts, w), F32), pltpu.VMEM((ts, w), F32),
                        pltpu.VMEM((ts, w), F32), pltpu.VMEM((1, w), F32)],
        compiler_params=_params("parallel", "arbitrary"),
        name="rglru",
    )(za, zg, conv_w, conv_b, wa_bd, ba, wx_bd, bx, lam)


def _compress_kernel(x_ref, pos_ref, w1_ref, w2_ref, gain_ref, o_ref, xpad):
    seq = x_ref.shape[0]
    n_out = seq // CMP_STRIDE
    j = pl.program_id(1)
    xpad[0:seq, :] = x_ref[...]
    xpad[seq:, :] = jnp.zeros((CMP_BLOCK, HEAD_DIM), F32)
    hidden = jnp.zeros((n_out, w1_ref.shape[2]), F32)
    for r in range(CMP_BLOCK):
        rows = xpad[pl.ds(r, n_out, stride=CMP_STRIDE), :] + pos_ref[0, r:r + 1, :]
        hidden = hidden + _dot(rows.astype(BF16), w1_ref[0, r * HEAD_DIM:(r + 1) * HEAD_DIM, :])
    hidden = hidden * _sigmoid(hidden)
    out = _dot(hidden.astype(BF16), w2_ref[0])
    out = jnp.where(j < NSA_KV_HEADS, _rms(out, gain_ref[...]), out)
    row = lax.broadcasted_iota(jnp.int32, out.shape, 0)
    o_ref[0, 0] = jnp.where(row < n_out - 1, out, 0.0).astype(o_ref.dtype)


def _compress(zc, cmp_pos, cmp_w1, cmp_w2, k_gain, *, batch, seq):
    n_out = seq // CMP_STRIDE
    hid = cmp_w1.shape[2]
    return pl.pallas_call(
        _compress_kernel,
        out_shape=jax.ShapeDtypeStruct((batch, 2 * NSA_KV_HEADS, n_out, HEAD_DIM), BF16),
        grid=(batch, 2 * NSA_KV_HEADS),
        in_specs=[pl.BlockSpec((seq, HEAD_DIM), lambda b, j: (b, j)),
                  pl.BlockSpec((1, CMP_BLOCK, HEAD_DIM), lambda b, j: (j // NSA_KV_HEADS, 0, 0)),
                  pl.BlockSpec((1, CMP_BLOCK * HEAD_DIM, hid), lambda b, j: (j // NSA_KV_HEADS, 0, 0)),
                  pl.BlockSpec((1, hid, HEAD_DIM), lambda b, j: (j // NSA_KV_HEADS, 0, 0)),
                  pl.BlockSpec((1, HEAD_DIM), lambda b, j: (0, 0))],
        out_specs=pl.BlockSpec((1, 1, n_out, HEAD_DIM), lambda b, j: (b, j, 0, 0)),
        scratch_shapes=[pltpu.VMEM((seq + CMP_BLOCK, HEAD_DIM), F32)],
        compiler_params=_params("parallel", "arbitrary"),
        name="compress",
    )(zc, cmp_pos, cmp_w1, cmp_w2, k_gain)


FOX_BIAS_TERMS = 3


def _forget_prefix_kernel(s_ref, o_ref, *, heads):
    seq = s_ref.shape[0]
    f = s_ref[...]
    log_f = jnp.minimum(f, 0.0) - jnp.log(1.0 + jnp.exp(-jnp.abs(f)))
    r = lax.broadcasted_iota(jnp.int32, (LANES, LANES), 0)
    c = lax.broadcasted_iota(jnp.int32, (LANES, LANES), 1)
    lower = jnp.where(c <= r, 1.0, 0.0).astype(BF16)
    pr = lax.broadcasted_iota(jnp.int32, (LANES, heads * LANES), 0)
    pc = lax.broadcasted_iota(jnp.int32, (LANES, heads * LANES), 1)
    place = [jnp.where((pc == pr * LANES + j) & (pr < heads), 1.0, 0.0).astype(BF16) for j in range(FOX_BIAS_TERMS)]
    carry = jnp.zeros((1, LANES), F32)
    for k in range(seq // LANES):
        rows = slice(k * LANES, (k + 1) * LANES)
        hi, mid, lo = _split3(log_f[rows])
        chunk = (_dot(lower, lo) + _dot(lower, mid)) + _dot(lower, hi) + carry
        carry = chunk[LANES - 1:LANES, :]
        terms = _split3(chunk * (-LOG2E))
        out = _dot(terms[0], place[0])
        for j in range(1, FOX_BIAS_TERMS):
            out = out + _dot(terms[j], place[j])
        o_ref[rows, :] = out.astype(o_ref.dtype)


def _forget_prefix(zc, *, batch, seq, heads, col_block):
    return pl.pallas_call(
        functools.partial(_forget_prefix_kernel, heads=heads),
        out_shape=jax.ShapeDtypeStruct((batch * seq, heads * LANES), BF16),
        grid=(batch,),
        in_specs=[pl.BlockSpec((seq, LANES), lambda b: (b, col_block))],
        out_specs=pl.BlockSpec((seq, heads * LANES), lambda b: (b, 0)),
        compiler_params=_params("parallel"),
        name="forget_prefix",
    )(zc)


NSA_TQ = 256
NSA_TK = 512
SEL_SHIFT = SEL_BLOCK.bit_length() - 1


def _biased_exp2(s, bias):
    rows, k = s.shape
    y = (s.reshape(NSA_GROUP, rows // NSA_GROUP, k) + bias[None]).reshape(rows, k)
    e = jnp.exp2(y - jnp.max(y, axis=-1, keepdims=True))
    return e, jnp.sum(e, axis=-1, keepdims=True)


def _top_rows_bias(score, n_top):
    n, t = score.shape
    n_groups = n // SUBLANES
    groups = [score[r * SUBLANES:(r + 1) * SUBLANES] for r in range(n_groups)]
    ranks = [jnp.zeros((SUBLANES, t), F32) for _ in range(n_groups)]
    local = lax.broadcasted_iota(jnp.int32, (SUBLANES, t), 0)
    for i in range(n):
        gi, li = divmod(i, SUBLANES)
        s_i = groups[gi][li:li + 1, :]
        for r in range(n_groups):
            if r > gi:
                ahead = s_i >= groups[r]
            elif r < gi:
                ahead = s_i > groups[r]
            else:
                ahead = (s_i > groups[r]) | ((s_i == groups[r]) & (local > li))
            ranks[r] = ranks[r] + jnp.where(ahead, 1.0, 0.0)
    return jnp.where(jnp.concatenate(ranks, axis=0) < n_top, 0.0, MASKED)


def _nsa_kernel(q_ref, kvc_ref, ks0_ref, ks1_ref, kw0_ref, kw1_ref, vs0_ref, vs1_ref, vw0_ref, vw1_ref,
                small_ref, gate_ref, o_ref, ks_aug, *, n_top, bgate_lane):
    tq = NSA_TQ
    rows = NSA_GROUP * tq
    seq = ks0_ref.shape[0]
    n_cmp_pad = kvc_ref.shape[2]
    n_sel = seq // SEL_BLOCK
    t0 = pl.multiple_of(pl.program_id(1) * tq, tq)
    branch_gate = _sigmoid(small_ref[...])

    @pl.when(pl.program_id(1) == 0)
    def _():
        blk_of_key = lax.broadcasted_iota(jnp.int32, (seq, LANES), 0) >> SEL_SHIFT
        one_hot = jnp.where(blk_of_key == lax.broadcasted_iota(jnp.int32, (seq, LANES), 1), 1.0, 0.0).astype(BF16)
        for g, ks_ref in enumerate((ks0_ref, ks1_ref)):
            ks_aug[g, :, 0:HEAD_DIM] = ks_ref[...]
            ks_aug[g, :, HEAD_DIM:] = one_hot

    tok_col = t0 + lax.broadcasted_iota(jnp.int32, (tq, 1), 0)
    tok_rows = jnp.broadcast_to(tok_col[None], (NSA_GROUP, tq, 1)).reshape(rows, 1)

    for g, (ks_ref, kw_ref, vs_ref, vw_ref) in enumerate(
            ((ks0_ref, kw0_ref, vs0_ref, vw0_ref), (ks1_ref, kw1_ref, vs1_ref, vw1_ref))):
        q4 = jnp.concatenate(
            [q_ref[:, (g * NSA_GROUP + h) * HEAD_DIM:(g * NSA_GROUP + h + 1) * HEAD_DIM] for h in range(NSA_GROUP)],
            axis=0)

        kc = kvc_ref[0, g]
        vc = kvc_ref[0, NSA_KV_HEADS + g]
        cmp_end = lax.broadcasted_iota(jnp.int32, (1, n_cmp_pad), 1) * CMP_STRIDE + (CMP_BLOCK - 1)
        e_c, sum_c = _biased_exp2(_dot_t(q4, kc), jnp.where(cmp_end <= tok_col, 0.0, MASKED))
        p_c = e_c * jnp.where(tok_rows >= CMP_BLOCK - 1, 1.0 / sum_c, 0.0)
        o_c = _dot(p_c.astype(BF16), vc)

        p_sum = p_c[0:tq]
        for h in range(1, NSA_GROUP):
            p_sum = p_sum + p_c[h * tq:(h + 1) * tq]
        jj = lax.broadcasted_iota(jnp.int32, (n_sel, n_cmp_pad), 0)
        cc = lax.broadcasted_iota(jnp.int32, (n_sel, n_cmp_pad), 1)
        overlap_t = jnp.where((cc * CMP_STRIDE < (jj + 1) * SEL_BLOCK) & (cc * CMP_STRIDE + CMP_BLOCK > jj * SEL_BLOCK)
                              & (cc < n_cmp_pad - 1), 1.0, 0.0).astype(BF16)
        hi, mid, lo = _split3(p_sum)
        imp_t = (_dot_t(overlap_t, lo) + _dot_t(overlap_t, mid)) + _dot_t(overlap_t, hi)
        blk = lax.broadcasted_iota(jnp.int32, (n_sel, tq), 0)
        cur = (t0 + lax.broadcasted_iota(jnp.int32, (n_sel, tq), 1)) >> SEL_SHIFT
        forced = (blk == 0) | (blk == cur) | (blk == cur - 1)
        score = jnp.where(forced, FORCED_SCORE, jnp.where(blk <= cur, imp_t, INVALID_SCORE))
        sel_bias = _top_rows_bias(score, n_top)
        sel_bias = jnp.concatenate([sel_bias, jnp.zeros((LANES - n_sel, tq), F32)], axis=0).T.astype(BF16)
        q_aug = jnp.concatenate([q4, jnp.concatenate([sel_bias] * NSA_GROUP, axis=0)], axis=1)

        def sel_tile(kt, carry, causal):
            m, l, acc = carry
            k0 = pl.multiple_of(kt * NSA_TK, NSA_TK)
            y = _dot_t(q_aug, ks_aug[g, pl.ds(k0, NSA_TK), :])
            if causal:
                kpos = k0 + lax.broadcasted_iota(jnp.int32, (1, NSA_TK), 1)
                y = (y.reshape(NSA_GROUP, tq, NSA_TK) + jnp.where(kpos <= tok_col, 0.0, MASKED)[None]).reshape(
                    rows, NSA_TK)
            m_new = jnp.maximum(m, jnp.max(y, axis=-1, keepdims=True))
            alpha = jnp.exp2(m - m_new)
            e = jnp.exp2(y - m_new)
            l = alpha * l + jnp.sum(e, axis=-1, keepdims=True)
            acc = alpha * acc + _dot(e.astype(BF16), vs_ref[pl.ds(k0, NSA_TK), :])
            return m_new, l, acc

        kd = t0 // NSA_TK
        init = (jnp.full((rows, 1), 3.0 * MASKED, F32), jnp.zeros((rows, 1), F32),
                jnp.zeros((rows, HEAD_DIM), F32))

        def sel_pair(j, carry):
            return sel_tile(2 * j + 1, sel_tile(2 * j, carry, False), False)

        carry = lax.fori_loop(0, kd // 2, sel_pair, init)
        carry = lax.fori_loop(kd - kd % 2, kd, functools.partial(sel_tile, causal=False), carry)
        _, l, acc = sel_tile(kd, carry, True)
        o_s = acc * (1.0 / l)

        w0 = pl.multiple_of(jnp.maximum(t0 - WINDOW, 0), tq)
        wpos = w0 + lax.broadcasted_iota(jnp.int32, (1, WINDOW + tq), 1)
        e_w, sum_w = _biased_exp2(_dot_t(q4, kw_ref[pl.ds(w0, WINDOW + tq), :]),
                                  jnp.where((wpos <= tok_col) & (wpos > tok_col - WINDOW), 0.0, MASKED))
        o_w = _dot(e_w.astype(BF16), vw_ref[pl.ds(w0, WINDOW + tq), :]) * (1.0 / sum_w)

        for h in range(NSA_GROUP):
            head = g * NSA_GROUP + h
            r = slice(h * tq, (h + 1) * tq)
            lane = bgate_lane + head * N_BRANCH
            mix = (branch_gate[:, lane:lane + 1] * o_c[r] + branch_gate[:, lane + 1:lane + 2] * o_s[r]
                   + branch_gate[:, lane + 2:lane + 3] * o_w[r])
            cols = slice(head * HEAD_DIM, (head + 1) * HEAD_DIM)
            o_ref[:, cols] = (mix * gate_ref[:, cols]).astype(o_ref.dtype)


def _nsa(zq, zv, kvc, zc, zg, *, batch, seq, cols):
    nq = seq // NSA_TQ
    width = NSA_KV_HEADS * NSA_GROUP * HEAD_DIM
    n_top = min(SEL_TOPK, seq // SEL_BLOCK)
    row = lambda b, i: b * nq + i
    kv_spec = lambda c: pl.BlockSpec((seq, HEAD_DIM), lambda b, i: (b, c))
    return pl.pallas_call(
        functools.partial(_nsa_kernel, n_top=n_top, bgate_lane=cols["bgate_lane"]),
        out_shape=jax.ShapeDtypeStruct((batch * seq, width), BF16),
        grid=(batch, nq),
        in_specs=[pl.BlockSpec((NSA_TQ, width), lambda b, i: (row(b, i), cols["nsa_q"] * HEAD_DIM // width)),
                  pl.BlockSpec((1,) + kvc.shape[1:], lambda b, i: (b, 0, 0, 0)),
                  kv_spec(cols["k_slc"]), kv_spec(cols["k_slc"] + 1),
                  kv_spec(cols["k_win"]), kv_spec(cols["k_win"] + 1),
                  kv_spec(cols["v_slc"]), kv_spec(cols["v_slc"] + 1),
                  kv_spec(cols["v_win"]), kv_spec(cols["v_win"] + 1),
                  pl.BlockSpec((NSA_TQ, LANES), lambda b, i: (row(b, i), cols["small"])),
                  pl.BlockSpec((NSA_TQ, width), lambda b, i: (row(b, i), cols["nsa_gate"] * HEAD_DIM // width))],
        out_specs=pl.BlockSpec((NSA_TQ, width), lambda b, i: (row(b, i), 0)),
        scratch_shapes=[pltpu.VMEM((NSA_KV_HEADS, seq, HEAD_DIM + LANES), BF16)],
        compiler_params=_params("parallel", "arbitrary"),
        name="sparse_attention",
    )(zq, kvc, zq, zq, zq, zq, zv, zv, zv, zv, zc, zg)


FOX_T = 1024
FOX_TK = 512


def _fox_kernel(q_ref, k_ref, kb_ref, v_ref, gate_ref, o_ref, k_aug):
    t, tk = FOX_T, FOX_TK
    i = pl.program_id(2)
    q0 = i * t

    @pl.when(i == 0)
    def _():
        k_aug[:, 0:HEAD_DIM] = k_ref[...]
        k_aug[:, HEAD_DIM:] = kb_ref[...]

    lane = lax.broadcasted_iota(jnp.int32, (t, LANES), 1)
    q_aug = jnp.concatenate([q_ref[...], jnp.where(lane < FOX_BIAS_TERMS, 1.0, 0.0).astype(BF16)], axis=1)

    def tile(kt, carry, causal=False):
        m, l, acc = carry
        k0 = pl.multiple_of(kt * tk, tk)
        y = _dot_t(q_aug, k_aug[pl.ds(k0, tk), :])
        if causal:
            qpos = q0 + lax.broadcasted_iota(jnp.int32, (t, tk), 0)
            kpos = k0 + lax.broadcasted_iota(jnp.int32, (t, tk), 1)
            y = jnp.where(kpos <= qpos, y, MASKED)
        m_new = jnp.maximum(m, jnp.max(y, axis=-1, keepdims=True))
        alpha = jnp.exp2(m - m_new)
        e = jnp.exp2(y - m_new)
        l = alpha * l + jnp.sum(e, axis=-1, keepdims=True)
        acc = alpha * acc + _dot(e.astype(BF16), v_ref[pl.ds(k0, tk), :])
        return m_new, l, acc

    per_q = t // tk
    n_full = i * per_q

    def tile_group(j, carry):
        for d in range(per_q):
            carry = tile(j * per_q + d, carry)
        return carry

    carry = (jnp.full((t, 1), MASKED, F32), jnp.zeros((t, 1), F32), jnp.zeros((t, HEAD_DIM), F32))
    carry = lax.fori_loop(0, i, tile_group, carry)
    for d in range(per_q):
        carry = tile(n_full + d, carry, causal=True)
    _, l, acc = carry
    o_ref[...] = (acc * (1.0 / l) * gate_ref[...]).astype(o_ref.dtype)


def _fox(zq, zv, kbias, zg, *, batch, seq, heads, cols):
    nq = seq // FOX_T
    row = lambda b, h, i: b * nq + i
    return pl.pallas_call(
        _fox_kernel,
        out_shape=jax.ShapeDtypeStruct((batch * seq, heads * HEAD_DIM), BF16),
        grid=(batch, heads, nq),
        in_specs=[pl.BlockSpec((FOX_T, HEAD_DIM), lambda b, h, i: (row(b, h, i), cols["fox_q"] + h)),
                  pl.BlockSpec((seq, HEAD_DIM), lambda b, h, i: (b, cols["fox_k"] + h)),
                  pl.BlockSpec((seq, LANES), lambda b, h, i: (b, h)),
                  pl.BlockSpec((seq, HEAD_DIM), lambda b, h, i: (b, cols["fox_v"] + h)),
                  pl.BlockSpec((FOX_T, HEAD_DIM), lambda b, h, i: (row(b, h, i), cols["fox_gate"] + h))],
        out_specs=pl.BlockSpec((FOX_T, HEAD_DIM), lambda b, h, i: (row(b, h, i), h)),
        scratch_shapes=[pltpu.VMEM((seq, HEAD_DIM + LANES), BF16)],
        compiler_params=_params("parallel", "parallel", "arbitrary"),
        name="forgetting_attention",
    )(zq, zq, kbias, zv, zg)


def _merge_kernel(ya_ref, yb_ref, yc_ref, w_ref, ga_ref, gb_ref, gc_ref, o_ref):
    tn = o_ref.shape[1]
    step = MXU_WIDTH if tn % MXU_WIDTH == 0 else tn
    ya, yb, yc = ya_ref[...], yb_ref[...], yc_ref[...]
    for c0 in range(0, tn, step):
        cols = slice(c0, c0 + step)
        merged = ga_ref[:, cols] * _dot(ya, w_ref[0, :, cols])
        merged = merged + gb_ref[:, cols] * _dot(yb, w_ref[1, :, cols])
        merged = merged + gc_ref[:, cols] * _dot(yc, w_ref[2, :, cols])
        o_ref[:, cols] = merged.astype(o_ref.dtype)


def _merge(ya, yb, yc, w_branch, zm, *, tm, tn):
    m, k = ya.shape
    n = w_branch.shape[2]
    y_spec = pl.BlockSpec((tm, k), lambda i, j: (i, 0))
    g_spec = lambda br: pl.BlockSpec((tm, tn), lambda i, j: (i, br * (n // tn) + j))
    return pl.pallas_call(
        _merge_kernel,
        out_shape=jax.ShapeDtypeStruct((m, n), BF16),
        grid=(m // tm, n // tn),
        in_specs=[y_spec, y_spec, y_spec, pl.BlockSpec((N_BRANCH, k, tn), lambda i, j: (0, 0, j)),
                  g_spec(0), g_spec(1), g_spec(2)],
        out_specs=pl.BlockSpec((tm, tn), lambda i, j: (i, j)),
        compiler_params=_params("parallel", "arbitrary"),
        name="merge",
    )(ya, yb, yc, w_branch, zm, zm, zm)


def _ple_kernel(h_ref, p_ref, wp_ref, pg_ref, wg_ref, ng_ref, h_out_ref, hn_out_ref):
    h = h_ref[...]
    e = _rms(_dot(p_ref[...].astype(BF16), wp_ref[...]), pg_ref[...])
    h = h + _sigmoid(_dot(h.astype(BF16), wg_ref[...])) * e
    h_out_ref[...] = h
    hn_out_ref[...] = _rms(h, ng_ref[...]).astype(BF16)


def _ple(h, p, w_ple, ple_gain, w_gate, next_gain, *, tm):
    t, d = h.shape
    pd = p.shape[1]
    const = lambda i: (0, 0)
    return pl.pallas_call(
        _ple_kernel,
        out_shape=(jax.ShapeDtypeStruct((t, d), F32), jax.ShapeDtypeStruct((t, d), BF16)),
        grid=(t // tm,),
        in_specs=[pl.BlockSpec((tm, d), lambda i: (i, 0)), pl.BlockSpec((tm, pd), lambda i: (i, 0)),
                  pl.BlockSpec((pd, d), const), pl.BlockSpec((1, d), const),
                  pl.BlockSpec((d, d), const), pl.BlockSpec((1, d), const)],
        out_specs=(pl.BlockSpec((tm, d), lambda i: (i, 0)), pl.BlockSpec((tm, d), lambda i: (i, 0))),
        compiler_params=_params("parallel"),
        name="ple_update",
    )(h, p, w_ple, ple_gain, w_gate, next_gain)


def _in_proj_layout(d_model, branch_width):
    kvw = NSA_KV_HEADS * HEAD_DIM
    heads = branch_width // HEAD_DIM
    splits = (("lru_x", branch_width), ("lru_gate", branch_width), ("nsa_q", branch_width),
              ("nsa_k_cmp", kvw), ("nsa_v_cmp", kvw), ("nsa_k_slc", kvw), ("nsa_v_slc", kvw),
              ("nsa_k_win", kvw), ("nsa_v_win", kvw), ("nsa_bgate", heads * N_BRANCH),
              ("nsa_gate", branch_width), ("fox_q", branch_width), ("fox_k", branch_width),
              ("fox_v", branch_width), ("fox_f", heads), ("fox_gate", branch_width),
              ("merge", N_BRANCH * d_model))
    offs, start = {}, 0
    for name, size in splits:
        offs[name] = (start, size)
        start += size
    return offs


_GROUPS = {
    "za": (("lru_x",), "none", F32),
    "zg": (("lru_gate", "nsa_gate", "fox_gate"), "silu", BF16),
    "zq": (("nsa_q", "nsa_k_slc", "nsa_k_win", "fox_q", "fox_k"), "headnorm", BF16),
    "zv": (("nsa_v_slc", "nsa_v_win", "fox_v"), "none", BF16),
    "zc": (("nsa_k_cmp", "nsa_v_cmp", "fox_f", "nsa_bgate"), "none", F32),
    "zm": (("merge",), "sigmoid", BF16),
}
_PROJ_TILES = {"headnorm": (1024, 1792)}
_PROJ_TILES_DEFAULT = (2048, 1536)


def _gather_cols(w, offs, names, pad_to=None):
    parts = [lax.slice_in_dim(w, offs[n][0], offs[n][0] + offs[n][1], axis=w.ndim - 1) for n in names]
    out = jnp.concatenate(parts, axis=-1) if len(parts) > 1 else parts[0]
    if pad_to is not None and out.shape[-1] % pad_to:
        pad = pad_to - out.shape[-1] % pad_to
        out = jnp.pad(out, [(0, 0)] * (out.ndim - 1) + [(0, pad)])
    return out


def _block_diag(w, pack):
    depth, nb, k, _ = w.shape
    per = pack // k
    w = w.reshape(depth, nb // per, per, k, k)
    eye = jnp.eye(per, dtype=w.dtype)
    return jnp.einsum("dgpij,pq->dgpiqj", w, eye).reshape(depth, nb // per, pack, pack)


def _tile(n, target):
    if n <= target:
        return n
    for unit in (MXU_WIDTH, LANES):
        fits = [t for t in range(unit, target + 1, unit) if n % t == 0]
        if fits:
            return fits[-1]
    return LANES


def kernel(x, p, ln_gain, w_in, b_in, conv_w, conv_b, lru_wa, lru_ba, lru_wx, lru_bx, lru_lambda, cmp_w1, cmp_w2,
           cmp_pos, nsa_q_gain, nsa_k_gain, fox_q_gain, fox_k_gain, w_branch, w_out, w_ple, ple_gain, w_ple_gate):
    batch, seq, d_model = x.shape
    depth = w_in.shape[0]
    bw = conv_w.shape[2]
    heads = bw // HEAD_DIM
    t = batch * seq
    assert seq % NSA_TK == 0 and seq % FOX_T == 0 and seq >= WINDOW + NSA_TQ and seq // SEL_BLOCK <= LANES
    offs = _in_proj_layout(d_model, bw)

    wg, bg = {}, {}
    for name, (members, _, _) in _GROUPS.items():
        wg[name] = _gather_cols(w_in, offs, members, pad_to=LANES).astype(BF16)
        bg[name] = _gather_cols(b_in, offs, members, pad_to=LANES)[:, None, :]
    rep = lambda g, n: jnp.tile(g, (1, n))
    q_scale = HEAD_DIM ** -0.5 * LOG2E
    zq_gain = jnp.concatenate([rep(nsa_q_gain * q_scale, heads), rep(nsa_k_gain, 2 * NSA_KV_HEADS),
                               rep(fox_q_gain * q_scale, heads), rep(fox_k_gain, heads)], axis=1)[:, None, :]
    hb = bw // HEAD_DIM
    cols = {"nsa_q": 0, "k_slc": hb, "k_win": hb + NSA_KV_HEADS, "fox_q": hb + 2 * NSA_KV_HEADS,
            "fox_k": 2 * hb + 2 * NSA_KV_HEADS,
            "v_slc": 0, "v_win": NSA_KV_HEADS, "fox_v": 2 * NSA_KV_HEADS,
            "small": 2 * NSA_KV_HEADS, "bgate_lane": heads,
            "nsa_gate": hb, "fox_gate": 2 * hb}
    wa_bd = _block_diag(lru_wa, LRU_PACK).astype(BF16)
    wx_bd = _block_diag(lru_wx, LRU_PACK).astype(BF16)
    w_branch_b = w_branch.astype(BF16)
    w_out_b = w_out.astype(BF16)
    w_ple_b = w_ple.astype(BF16)
    w_gate_b = w_ple_gate.astype(BF16)
    cmp_w1_b = cmp_w1.astype(BF16)
    cmp_w2_b = cmp_w2.astype(BF16)
    zero_bias = jnp.zeros((1, d_model), F32)

    tm = _tile(t, 1024)
    h = x.reshape(t, d_model)
    p2 = p.reshape(depth, t, p.shape[-1])
    hn = _prenorm(h, ln_gain[0][None], _tile(t, 512))
    for l in range(depth):
        z = {}
        for name, (_, epilogue, dtype) in _GROUPS.items():
            n = wg[name].shape[2]
            tm_target, tn_target = _PROJ_TILES.get(epilogue, _PROJ_TILES_DEFAULT)
            z[name] = _matmul(hn, wg[name][l], bg[name][l], epilogue=epilogue, out_dtype=dtype, tm=_tile(t, tm_target),
                              tn=_tile(n, tn_target), extra=zq_gain[l] if epilogue == "headnorm" else None,
                              name="in_proj_" + name)
        ya = _rglru(z["za"], z["zg"], conv_w[l], conv_b[l][None], wa_bd[l], lru_ba[l][None], wx_bd[l],
                    lru_bx[l][None], lru_lambda[l][None], batch=batch, seq=seq, ts=_tile(seq, 512))
        kvc = _compress(z["zc"], cmp_pos[l], cmp_w1_b[l], cmp_w2_b[l], nsa_k_gain[l][None], batch=batch, seq=seq)
        yb = _nsa(z["zq"], z["zv"], kvc, z["zc"], z["zg"], batch=batch, seq=seq, cols=cols)
        kbias = _forget_prefix(z["zc"], batch=batch, seq=seq, heads=heads, col_block=cols["small"])
        yc = _fox(z["zq"], z["zv"], kbias, z["zg"], batch=batch, seq=seq, heads=heads, cols=cols)
        merged = _merge(ya, yb, yc, w_branch_b[l], z["zm"], tm=_tile(t, 512), tn=d_model)
        h = _matmul(merged, w_out_b[l], zero_bias, epilogue="residual", out_dtype=F32, tm=_tile(t, 512),
                    tn=d_model, extra=h, name="out_proj")
        next_gain = ln_gain[(l + 1) % depth][None]
        h, hn = _ple(h, p2[l], w_ple_b[l], ple_gain[l][None], w_gate_b[l], next_gain, tm=_tile(t, 512))
    return h.reshape(batch, seq, d_model)
```

```python
import functools

import jax
import jax.numpy as jnp
from jax import lax
from jax.experimental import pallas as pl
from jax.experimental.pallas import tpu as pltpu

F32 = jnp.float32
BF16 = jnp.bfloat16

HEAD_DIM = 128
NORM_EPS = 1e-6
LRU_BLOCK = 64
CONV_WIDTH = 4
LRU_C = 8.0
NSA_KV_HEADS = 2
NSA_GROUP = 4
CMP_BLOCK = 32
CMP_STRIDE = 16
SEL_BLOCK = 64
SEL_TOPK = 16
WINDOW = 512
N_BRANCH = 3

V7X_VMEM_BYTES = 64 * 1024 * 1024
VMEM_LIMIT = V7X_VMEM_BYTES - 8 * 1024 * 1024
LANES = 128
SUBLANES = 8
MXU_WIDTH = 256
LOG2E = 1.4426950408889634

MASKED = -1e30
FORCED_SCORE = 1e30
INVALID_SCORE = -1.0


def _params(*semantics):
    return pltpu.CompilerParams(dimension_semantics=semantics, vmem_limit_bytes=VMEM_LIMIT)


def _dot(a, b):
    return jnp.dot(a, b, preferred_element_type=F32)


def _dot_t(a, b):
    return lax.dot_general(a, b, (((1,), (1,)), ((), ())), preferred_element_type=F32)


def _split3(x):
    hi = x.astype(BF16)
    r1 = x - hi.astype(F32)
    mid = r1.astype(BF16)
    lo = (r1 - mid.astype(F32)).astype(BF16)
    return hi, mid, lo


def _sigmoid(x):
    return 0.5 * jnp.tanh(0.5 * x) + 0.5


def _softplus(x):
    return jnp.maximum(x, 0.0) + jnp.log(1.0 + jnp.exp(-jnp.abs(x)))


def _rms(x, gain):
    y = x * lax.rsqrt(jnp.mean(x * x, axis=-1, keepdims=True) + NORM_EPS)
    return y * gain


def _prenorm_kernel(h_ref, g_ref, o_ref):
    o_ref[...] = _rms(h_ref[...], g_ref[...]).astype(BF16)


def _prenorm(h, gain, tm):
    t, d = h.shape
    return pl.pallas_call(
        _prenorm_kernel,
        out_shape=jax.ShapeDtypeStruct((t, d), BF16),
        grid=(t // tm,),
        in_specs=[pl.BlockSpec((tm, d), lambda i: (i, 0)), pl.BlockSpec((1, d), lambda i: (0, 0))],
        out_specs=pl.BlockSpec((tm, d), lambda i: (i, 0)),
        compiler_params=_params("parallel"),
        name="prenorm",
    )(h, gain)


def _matmul_kernel(*refs, epilogue):
    if epilogue in ("headnorm", "residual"):
        x_ref, w_ref, b_ref, e_ref, o_ref = refs
    else:
        x_ref, w_ref, b_ref, o_ref = refs
    tn = o_ref.shape[1]
    step = MXU_WIDTH if tn % MXU_WIDTH == 0 else tn
    x = x_ref[...]
    for c0 in range(0, tn, step):
        cols = slice(c0, c0 + step)
        acc = _dot(x, w_ref[:, cols]) + b_ref[:, cols]
        if epilogue == "silu":
            acc = acc * _sigmoid(acc)
        elif epilogue == "sigmoid":
            acc = _sigmoid(acc)
        elif epilogue == "residual":
            acc = acc + e_ref[:, cols]
        if epilogue == "headnorm":
            for h0 in range(c0, c0 + step, HEAD_DIM):
                head = slice(h0, h0 + HEAD_DIM)
                o_ref[:, head] = _rms(acc[:, h0 - c0:h0 - c0 + HEAD_DIM], e_ref[:, head]).astype(o_ref.dtype)
        else:
            o_ref[:, cols] = acc.astype(o_ref.dtype)


def _matmul(x, w, b, *, layer, epilogue, out_dtype, tm, tn, extra=None, name):
    m, k = x.shape
    n = w.shape[2]
    in_specs = [pl.BlockSpec((tm, k), lambda i, j: (i, 0)),
                pl.BlockSpec((None, k, tn), lambda i, j: (layer, 0, j)),
                pl.BlockSpec((1, tn), lambda i, j: (0, j))]
    args = [x, w, b]
    if epilogue == "headnorm":
        in_specs.append(pl.BlockSpec((1, tn), lambda i, j: (0, j)))
        args.append(extra)
    elif epilogue == "residual":
        in_specs.append(pl.BlockSpec((tm, tn), lambda i, j: (i, j)))
        args.append(extra)
    return pl.pallas_call(
        functools.partial(_matmul_kernel, epilogue=epilogue),
        out_shape=jax.ShapeDtypeStruct((m, n), out_dtype),
        grid=(m // tm, n // tn),
        in_specs=in_specs,
        out_specs=pl.BlockSpec((tm, tn), lambda i, j: (i, j)),
        compiler_params=_params("parallel", "arbitrary"),
        name=name,
    )(*args)


LRU_PACK = 256


def _rglru_kernel(u_ref, g_ref, cw_ref, cb_ref, wa_ref, ba_ref, wx_ref, bx_ref, lam_ref, o_ref,
                  ubuf, abuf, bbuf, hbuf, hcar):
    ts, w = u_ref.shape
    s_idx = pl.program_id(1)

    @pl.when(s_idx == 0)
    def _():
        ubuf[0:8, :] = jnp.zeros((8, w), F32)
        hcar[...] = jnp.zeros((1, w), F32)

    ubuf[8:, :] = u_ref[...]
    uc = cb_ref[...] + cw_ref[CONV_WIDTH - 1:CONV_WIDTH, :] * ubuf[8:, :]
    for d in range(1, CONV_WIDTH):
        uc = uc + cw_ref[CONV_WIDTH - 1 - d:CONV_WIDTH - d, :] * ubuf[pl.ds(8 - d, ts), :]
    ubuf[0:8, :] = ubuf[ts:ts + 8, :]

    ucb = uc.astype(BF16)
    neg_c_softplus = -LRU_C * _softplus(-lam_ref[...])
    for c in range(w // LRU_PACK):
        cols = slice(c * LRU_PACK, (c + 1) * LRU_PACK)
        rec = _sigmoid(_dot(ucb[:, cols], wa_ref[c]) + ba_ref[:, cols])
        inp = _sigmoid(_dot(ucb[:, cols], wx_ref[c]) + bx_ref[:, cols])
        log_a = rec * neg_c_softplus[:, cols]
        a = jnp.exp(log_a)
        th = jnp.tanh(log_a)
        one_minus_a2 = -2.0 * th / (1.0 - th)
        abuf[:, cols] = a
        bbuf[:, cols] = jnp.sqrt(one_minus_a2) * (inp * uc[:, cols])

    def step(t, h):
        h = abuf[pl.ds(t, 1), :] * h + bbuf[pl.ds(t, 1), :]
        hbuf[pl.ds(t, 1), :] = h
        return h

    hcar[...] = lax.fori_loop(0, ts, step, hcar[...], unroll=8)
    o_ref[...] = (hbuf[...] * g_ref[...]).astype(o_ref.dtype)


def _rglru(za, zg, conv_w, conv_b, wa_bd, ba, wx_bd, bx, lam, *, batch, seq, ts):
    w = conv_w.shape[1]
    ns = seq // ts
    row = lambda b, s: (b * ns + s, 0)
    const = lambda b, s: (0, 0)
    return pl.pallas_call(
        _rglru_kernel,
        out_shape=jax.ShapeDtypeStruct((batch * seq, w), BF16),
        grid=(batch, ns),
        in_specs=[pl.BlockSpec((ts, w), row), pl.BlockSpec((ts, w), row),
                  pl.BlockSpec((CONV_WIDTH, w), const), pl.BlockSpec((1, w), const),
                  pl.BlockSpec(wa_bd.shape, lambda b, s: (0, 0, 0)), pl.BlockSpec((1, w), const),
                  pl.BlockSpec(wx_bd.shape, lambda b, s: (0, 0, 0)), pl.BlockSpec((1, w), const),
                  pl.BlockSpec((1, w), const)],
        out_specs=pl.BlockSpec((ts, w), row),
        scratch_shapes=[pltpu.VMEM((ts + 8, w), F32), pltpu.VMEM((ts, w), F32), pltpu.VMEM((ts, w), F32),
                        pltpu.VMEM((ts, w), F32), pltpu.VMEM((1, w), F32)],
        compiler_params=_params("parallel", "arbitrary"),
        name="rglru",
    )(za, zg, conv_w, conv_b, wa_bd, ba, wx_bd, bx, lam)


def _compress_kernel(x_ref, pos_ref, w1_ref, w2_ref, gain_ref, o_ref, xpad):
    seq = x_ref.shape[0]
    n_out = seq // CMP_STRIDE
    j = pl.program_id(1)
    xpad[0:seq, :] = x_ref[...]
    xpad[seq:, :] = jnp.zeros((CMP_BLOCK, HEAD_DIM), F32)
    hidden = jnp.zeros((n_out, w1_ref.shape[2]), F32)
    for r in range(CMP_BLOCK):
        rows = xpad[pl.ds(r, n_out, stride=CMP_STRIDE), :] + pos_ref[0, r:r + 1, :]
        hidden = hidden + _dot(rows.astype(BF16), w1_ref[0, r * HEAD_DIM:(r + 1) * HEAD_DIM, :])
    hidden = hidden * _sigmoid(hidden)
    out = _dot(hidden.astype(BF16), w2_ref[0])
    out = jnp.where(j < NSA_KV_HEADS, _rms(out, gain_ref[...]), out)
    row = lax.broadcasted_iota(jnp.int32, out.shape, 0)
    o_ref[0, 0] = jnp.where(row < n_out - 1, out, 0.0).astype(o_ref.dtype)


def _compress(zc, cmp_pos, cmp_w1, cmp_w2, k_gain, *, batch, seq):
    n_out = seq // CMP_STRIDE
    hid = cmp_w1.shape[2]
    return pl.pallas_call(
        _compress_kernel,
        out_shape=jax.ShapeDtypeStruct((batch, 2 * NSA_KV_HEADS, n_out, HEAD_DIM), BF16),
        grid=(batch, 2 * NSA_KV_HEADS),
        in_specs=[pl.BlockSpec((seq, HEAD_DIM), lambda b, j: (b, j)),
                  pl.BlockSpec((1, CMP_BLOCK, HEAD_DIM), lambda b, j: (j // NSA_KV_HEADS, 0, 0)),
                  pl.BlockSpec((1, CMP_BLOCK * HEAD_DIM, hid), lambda b, j: (j // NSA_KV_HEADS, 0, 0)),
                  pl.BlockSpec((1, hid, HEAD_DIM), lambda b, j: (j // NSA_KV_HEADS, 0, 0)),
                  pl.BlockSpec((1, HEAD_DIM), lambda b, j: (0, 0))],
        out_specs=pl.BlockSpec((1, 1, n_out, HEAD_DIM), lambda b, j: (b, j, 0, 0)),
        scratch_shapes=[pltpu.VMEM((seq + CMP_BLOCK, HEAD_DIM), F32)],
        compiler_params=_params("parallel", "arbitrary"),
        name="compress",
    )(zc, cmp_pos, cmp_w1, cmp_w2, k_gain)


FOX_BIAS_TERMS = 3


def _forget_prefix_kernel(s_ref, o_ref, *, heads):
    seq = s_ref.shape[0]
    f = s_ref[...]
    log_f = jnp.minimum(f, 0.0) - jnp.log(1.0 + jnp.exp(-jnp.abs(f)))
    r = lax.broadcasted_iota(jnp.int32, (LANES, LANES), 0)
    c = lax.broadcasted_iota(jnp.int32, (LANES, LANES), 1)
    lower = jnp.where(c <= r, 1.0, 0.0).astype(BF16)
    pr = lax.broadcasted_iota(jnp.int32, (LANES, heads * LANES), 0)
    pc = lax.broadcasted_iota(jnp.int32, (LANES, heads * LANES), 1)
    place = [jnp.where((pc == pr * LANES + j) & (pr < heads), 1.0, 0.0).astype(BF16) for j in range(FOX_BIAS_TERMS)]
    carry = jnp.zeros((1, LANES), F32)
    for k in range(seq // LANES):
        rows = slice(k * LANES, (k + 1) * LANES)
        hi, mid, lo = _split3(log_f[rows])
        chunk = (_dot(lower, lo) + _dot(lower, mid)) + _dot(lower, hi) + carry
        carry = chunk[LANES - 1:LANES, :]
        terms = _split3(chunk * (-LOG2E))
        out = _dot(terms[0], place[0])
        for j in range(1, FOX_BIAS_TERMS):
            out = out + _dot(terms[j], place[j])
        o_ref[rows, :] = out.astype(o_ref.dtype)


def _forget_prefix(zc, *, batch, seq, heads, col_block):
    return pl.pallas_call(
        functools.partial(_forget_prefix_kernel, heads=heads),
        out_shape=jax.ShapeDtypeStruct((batch * seq, heads * LANES), BF16),
        grid=(batch,),
        in_specs=[pl.BlockSpec((seq, LANES), lambda b: (b, col_block))],
        out_specs=pl.BlockSpec((seq, heads * LANES), lambda b: (b, 0)),
        compiler_params=_params("parallel"),
        name="forget_prefix",
    )(zc)


NSA_TQ = 256
NSA_TK = 512
SEL_SHIFT = SEL_BLOCK.bit_length() - 1


def _biased_exp2(s, bias):
    rows, k = s.shape
    y = (s.reshape(NSA_GROUP, rows // NSA_GROUP, k) + bias[None]).reshape(rows, k)
    e = jnp.exp2(y - jnp.max(y, axis=-1, keepdims=True))
    return e, jnp.sum(e, axis=-1, keepdims=True)


def _top_rows_bias(score, n_top):
    n, t = score.shape
    n_groups = n // SUBLANES
    groups = [score[r * SUBLANES:(r + 1) * SUBLANES] for r in range(n_groups)]
    ranks = [jnp.zeros((SUBLANES, t), F32) for _ in range(n_groups)]
    local = lax.broadcasted_iota(jnp.int32, (SUBLANES, t), 0)
    for i in range(n):
        gi, li = divmod(i, SUBLANES)
        s_i = groups[gi][li:li + 1, :]
        for r in range(n_groups):
            if r > gi:
                ahead = s_i >= groups[r]
            elif r < gi:
                ahead = s_i > groups[r]
            else:
                ahead = (s_i > groups[r]) | ((s_i == groups[r]) & (local > li))
            ranks[r] = ranks[r] + jnp.where(ahead, 1.0, 0.0)
    return jnp.where(jnp.concatenate(ranks, axis=0) < n_top, 0.0, MASKED)


def _nsa_kernel(q_ref, kvc_ref, ks0_ref, ks1_ref, kw0_ref, kw1_ref, vs0_ref, vs1_ref, vw0_ref, vw1_ref,
                small_ref, gate_ref, o_ref, ks_aug, *, n_top, bgate_lane):
    tq = NSA_TQ
    rows = NSA_GROUP * tq
    seq = ks0_ref.shape[0]
    n_cmp_pad = kvc_ref.shape[2]
    n_sel = seq // SEL_BLOCK
    t0 = pl.multiple_of(pl.program_id(1) * tq, tq)
    branch_gate = _sigmoid(small_ref[...])

    @pl.when(pl.program_id(1) == 0)
    def _():
        blk_of_key = lax.broadcasted_iota(jnp.int32, (seq, LANES), 0) >> SEL_SHIFT
        one_hot = jnp.where(blk_of_key == lax.broadcasted_iota(jnp.int32, (seq, LANES), 1), 1.0, 0.0).astype(BF16)
        for g, ks_ref in enumerate((ks0_ref, ks1_ref)):
            ks_aug[g, :, 0:HEAD_DIM] = ks_ref[...]
            ks_aug[g, :, HEAD_DIM:] = one_hot

    tok_col = t0 + lax.broadcasted_iota(jnp.int32, (tq, 1), 0)
    tok_rows = jnp.broadcast_to(tok_col[None], (NSA_GROUP, tq, 1)).reshape(rows, 1)

    for g, (ks_ref, kw_ref, vs_ref, vw_ref) in enumerate(
            ((ks0_ref, kw0_ref, vs0_ref, vw0_ref), (ks1_ref, kw1_ref, vs1_ref, vw1_ref))):
        q4 = jnp.concatenate(
            [q_ref[:, (g * NSA_GROUP + h) * HEAD_DIM:(g * NSA_GROUP + h + 1) * HEAD_DIM] for h in range(NSA_GROUP)],
            axis=0)

        kc = kvc_ref[0, g]
        vc = kvc_ref[0, NSA_KV_HEADS + g]
        cmp_end = lax.broadcasted_iota(jnp.int32, (1, n_cmp_pad), 1) * CMP_STRIDE + (CMP_BLOCK - 1)
        e_c, sum_c = _biased_exp2(_dot_t(q4, kc), jnp.where(cmp_end <= tok_col, 0.0, MASKED))
        p_c = e_c * jnp.where(tok_rows >= CMP_BLOCK - 1, 1.0 / sum_c, 0.0)
        o_c = _dot(p_c.astype(BF16), vc)

        p_sum = p_c[0:tq]
        for h in range(1, NSA_GROUP):
            p_sum = p_sum + p_c[h * tq:(h + 1) * tq]
        jj = lax.broadcasted_iota(jnp.int32, (n_sel, n_cmp_pad), 0)
        cc = lax.broadcasted_iota(jnp.int32, (n_sel, n_cmp_pad), 1)
        overlap_t = jnp.where((cc * CMP_STRIDE < (jj + 1) * SEL_BLOCK) & (cc * CMP_STRIDE + CMP_BLOCK > jj * SEL_BLOCK)
                              & (cc < n_cmp_pad - 1), 1.0, 0.0).astype(BF16)
        hi, mid, lo = _split3(p_sum)
        imp_t = (_dot_t(overlap_t, lo) + _dot_t(overlap_t, mid)) + _dot_t(overlap_t, hi)
        blk = lax.broadcasted_iota(jnp.int32, (n_sel, tq), 0)
        cur = (t0 + lax.broadcasted_iota(jnp.int32, (n_sel, tq), 1)) >> SEL_SHIFT
        forced = (blk == 0) | (blk == cur) | (blk == cur - 1)
        score = jnp.where(forced, FORCED_SCORE, jnp.where(blk <= cur, imp_t, INVALID_SCORE))
        sel_bias = _top_rows_bias(score, n_top)
        sel_bias = jnp.concatenate([sel_bias, jnp.zeros((LANES - n_sel, tq), F32)], axis=0).T.astype(BF16)
        q_aug = jnp.concatenate([q4, jnp.concatenate([sel_bias] * NSA_GROUP, axis=0)], axis=1)

        def sel_tile(kt, carry, causal):
            m, l, acc = carry
            k0 = pl.multiple_of(kt * NSA_TK, NSA_TK)
            y = _dot_t(q_aug, ks_aug[g, pl.ds(k0, NSA_TK), :])
            if causal:
                kpos = k0 + lax.broadcasted_iota(jnp.int32, (1, NSA_TK), 1)
                y = (y.reshape(NSA_GROUP, tq, NSA_TK) + jnp.where(kpos <= tok_col, 0.0, MASKED)[None]).reshape(
                    rows, NSA_TK)
            m_new = jnp.maximum(m, jnp.max(y, axis=-1, keepdims=True))
            alpha = jnp.exp2(m - m_new)
            e = jnp.exp2(y - m_new)
            l = alpha * l + jnp.sum(e, axis=-1, keepdims=True)
            acc = alpha * acc + _dot(e.astype(BF16), vs_ref[pl.ds(k0, NSA_TK), :])
            return m_new, l, acc

        kd = t0 // NSA_TK
        init = (jnp.full((rows, 1), 3.0 * MASKED, F32), jnp.zeros((rows, 1), F32),
                jnp.zeros((rows, HEAD_DIM), F32))

        def sel_pair(j, carry):
            return sel_tile(2 * j + 1, sel_tile(2 * j, carry, False), False)

        carry = lax.fori_loop(0, kd // 2, sel_pair, init)
        carry = lax.fori_loop(kd - kd % 2, kd, functools.partial(sel_tile, causal=False), carry)
        _, l, acc = sel_tile(kd, carry, True)
        o_s = acc * (1.0 / l)

        w0 = pl.multiple_of(jnp.maximum(t0 - WINDOW, 0), tq)
        wpos = w0 + lax.broadcasted_iota(jnp.int32, (1, WINDOW + tq), 1)
        e_w, sum_w = _biased_exp2(_dot_t(q4, kw_ref[pl.ds(w0, WINDOW + tq), :]),
                                  jnp.where((wpos <= tok_col) & (wpos > tok_col - WINDOW), 0.0, MASKED))
        o_w = _dot(e_w.astype(BF16), vw_ref[pl.ds(w0, WINDOW + tq), :]) * (1.0 / sum_w)

        for h in range(NSA_GROUP):
            head = g * NSA_GROUP + h
            r = slice(h * tq, (h + 1) * tq)
            lane = bgate_lane + head * N_BRANCH
            mix = (branch_gate[:, lane:lane + 1] * o_c[r] + branch_gate[:, lane + 1:lane + 2] * o_s[r]
                   + branch_gate[:, lane + 2:lane + 3] * o_w[r])
            cols = slice(head * HEAD_DIM, (head + 1) * HEAD_DIM)
            o_ref[:, cols] = (mix * gate_ref[:, cols]).astype(o_ref.dtype)


def _nsa(zq, zv, kvc, zc, zg, *, batch, seq, cols):
    nq = seq // NSA_TQ
    width = NSA_KV_HEADS * NSA_GROUP * HEAD_DIM
    n_top = min(SEL_TOPK, seq // SEL_BLOCK)
    row = lambda b, i: b * nq + i
    kv_spec = lambda c: pl.BlockSpec((seq, HEAD_DIM), lambda b, i: (b, c))
    return pl.pallas_call(
        functools.partial(_nsa_kernel, n_top=n_top, bgate_lane=cols["bgate_lane"]),
        out_shape=jax.ShapeDtypeStruct((batch * seq, width), BF16),
        grid=(batch, nq),
        in_specs=[pl.BlockSpec((NSA_TQ, width), lambda b, i: (row(b, i), cols["nsa_q"] * HEAD_DIM // width)),
                  pl.BlockSpec((1,) + kvc.shape[1:], lambda b, i: (b, 0, 0, 0)),
                  kv_spec(cols["k_slc"]), kv_spec(cols["k_slc"] + 1),
                  kv_spec(cols["k_win"]), kv_spec(cols["k_win"] + 1),
                  kv_spec(cols["v_slc"]), kv_spec(cols["v_slc"] + 1),
                  kv_spec(cols["v_win"]), kv_spec(cols["v_win"] + 1),
                  pl.BlockSpec((NSA_TQ, LANES), lambda b, i: (row(b, i), cols["small"])),
                  pl.BlockSpec((NSA_TQ, width), lambda b, i: (row(b, i), cols["nsa_gate"] * HEAD_DIM // width))],
        out_specs=pl.BlockSpec((NSA_TQ, width), lambda b, i: (row(b, i), 0)),
        scratch_shapes=[pltpu.VMEM((NSA_KV_HEADS, seq, HEAD_DIM + LANES), BF16)],
        compiler_params=_params("parallel", "arbitrary"),
        name="sparse_attention",
    )(zq, kvc, zq, zq, zq, zq, zv, zv, zv, zv, zc, zg)


FOX_T = 1024
FOX_TK = 512


def _fox_kernel(q_ref, k_ref, kb_ref, v_ref, gate_ref, o_ref, k_aug):
    t, tk = FOX_T, FOX_TK
    i = pl.program_id(2)
    q0 = i * t

    @pl.when(i == 0)
    def _():
        k_aug[:, 0:HEAD_DIM] = k_ref[...]
        k_aug[:, HEAD_DIM:] = kb_ref[...]

    lane = lax.broadcasted_iota(jnp.int32, (t, LANES), 1)
    q_aug = jnp.concatenate([q_ref[...], jnp.where(lane < FOX_BIAS_TERMS, 1.0, 0.0).astype(BF16)], axis=1)

    def tile(kt, carry, causal=False):
        m, l, acc = carry
        k0 = pl.multiple_of(kt * tk, tk)
        y = _dot_t(q_aug, k_aug[pl.ds(k0, tk), :])
        if causal:
            qpos = q0 + lax.broadcasted_iota(jnp.int32, (t, tk), 0)
            kpos = k0 + lax.broadcasted_iota(jnp.int32, (t, tk), 1)
            y = jnp.where(kpos <= qpos, y, MASKED)
        m_new = jnp.maximum(m, jnp.max(y, axis=-1, keepdims=True))
        alpha = jnp.exp2(m - m_new)
        e = jnp.exp2(y - m_new)
        l = alpha * l + jnp.sum(e, axis=-1, keepdims=True)
        acc = alpha * acc + _dot(e.astype(BF16), v_ref[pl.ds(k0, tk), :])
        return m_new, l, acc

    per_q = t // tk
    n_full = i * per_q

    def tile_group(j, carry):
        for d in range(per_q):
            carry = tile(j * per_q + d, carry)
        return carry

    carry = (jnp.full((t, 1), MASKED, F32), jnp.zeros((t, 1), F32), jnp.zeros((t, HEAD_DIM), F32))
    carry = lax.fori_loop(0, i, tile_group, carry)
    for d in range(per_q):
        carry = tile(n_full + d, carry, causal=True)
    _, l, acc = carry
    o_ref[...] = (acc * (1.0 / l) * gate_ref[...]).astype(o_ref.dtype)


def _fox(zq, zv, kbias, zg, *, batch, seq, heads, cols):
    nq = seq // FOX_T
    row = lambda b, h, i: b * nq + i
    return pl.pallas_call(
        _fox_kernel,
        out_shape=jax.ShapeDtypeStruct((batch * seq, heads * HEAD_DIM), BF16),
        grid=(batch, heads, nq),
        in_specs=[pl.BlockSpec((FOX_T, HEAD_DIM), lambda b, h, i: (row(b, h, i), cols["fox_q"] + h)),
                  pl.BlockSpec((seq, HEAD_DIM), lambda b, h, i: (b, cols["fox_k"] + h)),
                  pl.BlockSpec((seq, LANES), lambda b, h, i: (b, h)),
                  pl.BlockSpec((seq, HEAD_DIM), lambda b, h, i: (b, cols["fox_v"] + h)),
                  pl.BlockSpec((FOX_T, HEAD_DIM), lambda b, h, i: (row(b, h, i), cols["fox_gate"] + h))],
        out_specs=pl.BlockSpec((FOX_T, HEAD_DIM), lambda b, h, i: (row(b, h, i), h)),
        scratch_shapes=[pltpu.VMEM((seq, HEAD_DIM + LANES), BF16)],
        compiler_params=_params("parallel", "parallel", "arbitrary"),
        name="forgetting_attention",
    )(zq, zq, kbias, zv, zg)


def _merge_kernel(ya_ref, yb_ref, yc_ref, w_ref, ga_ref, gb_ref, gc_ref, o_ref):
    tn = o_ref.shape[1]
    step = MXU_WIDTH if tn % MXU_WIDTH == 0 else tn
    ya, yb, yc = ya_ref[...], yb_ref[...], yc_ref[...]
    for c0 in range(0, tn, step):
        cols = slice(c0, c0 + step)
        merged = ga_ref[:, cols] * _dot(ya, w_ref[0, :, cols])
        merged = merged + gb_ref[:, cols] * _dot(yb, w_ref[1, :, cols])
        merged = merged + gc_ref[:, cols] * _dot(yc, w_ref[2, :, cols])
        o_ref[:, cols] = merged.astype(o_ref.dtype)


def _merge(ya, yb, yc, w_branch, zm, *, layer, tm, tn):
    m, k = ya.shape
    n = w_branch.shape[3]
    y_spec = pl.BlockSpec((tm, k), lambda i, j: (i, 0))
    g_spec = lambda br: pl.BlockSpec((tm, tn), lambda i, j: (i, br * (n // tn) + j))
    return pl.pallas_call(
        _merge_kernel,
        out_shape=jax.ShapeDtypeStruct((m, n), BF16),
        grid=(m // tm, n // tn),
        in_specs=[y_spec, y_spec, y_spec, pl.BlockSpec((None, N_BRANCH, k, tn), lambda i, j: (layer, 0, 0, j)),
                  g_spec(0), g_spec(1), g_spec(2)],
        out_specs=pl.BlockSpec((tm, tn), lambda i, j: (i, j)),
        compiler_params=_params("parallel", "arbitrary"),
        name="merge",
    )(ya, yb, yc, w_branch, zm, zm, zm)


def _ple_kernel(h_ref, p_ref, wp_ref, pg_ref, wg_ref, ng_ref, h_out_ref, hn_out_ref):
    h = h_ref[...]
    e = _rms(_dot(p_ref[...].astype(BF16), wp_ref[...]), pg_ref[...])
    h = h + _sigmoid(_dot(h.astype(BF16), wg_ref[...])) * e
    h_out_ref[...] = h
    hn_out_ref[...] = _rms(h, ng_ref[...]).astype(BF16)


def _ple(h, p, w_ple, ple_gain, w_gate, next_gain, *, layer, tm):
    t, d = h.shape
    pd = p.shape[1]
    const = lambda i: (0, 0)
    of_layer = lambda i: (layer, 0, 0)
    return pl.pallas_call(
        _ple_kernel,
        out_shape=(jax.ShapeDtypeStruct((t, d), F32), jax.ShapeDtypeStruct((t, d), BF16)),
        grid=(t // tm,),
        in_specs=[pl.BlockSpec((tm, d), lambda i: (i, 0)), pl.BlockSpec((tm, pd), lambda i: (i, 0)),
                  pl.BlockSpec((None, pd, d), of_layer), pl.BlockSpec((1, d), const),
                  pl.BlockSpec((None, d, d), of_layer), pl.BlockSpec((1, d), const)],
        out_specs=(pl.BlockSpec((tm, d), lambda i: (i, 0)), pl.BlockSpec((tm, d), lambda i: (i, 0))),
        compiler_params=_params("parallel"),
        name="ple_update",
    )(h, p, w_ple, ple_gain, w_gate, next_gain)


def _in_proj_layout(d_model, branch_width):
    kvw = NSA_KV_HEADS * HEAD_DIM
    heads = branch_width // HEAD_DIM
    splits = (("lru_x", branch_width), ("lru_gate", branch_width), ("nsa_q", branch_width),
              ("nsa_k_cmp", kvw), ("nsa_v_cmp", kvw), ("nsa_k_slc", kvw), ("nsa_v_slc", kvw),
              ("nsa_k_win", kvw), ("nsa_v_win", kvw), ("nsa_bgate", heads * N_BRANCH),
              ("nsa_gate", branch_width), ("fox_q", branch_width), ("fox_k", branch_width),
              ("fox_v", branch_width), ("fox_f", heads), ("fox_gate", branch_width),
              ("merge", N_BRANCH * d_model))
    offs, start = {}, 0
    for name, size in splits:
        offs[name] = (start, size)
        start += size
    return offs


_GROUPS = {
    "za": (("lru_x",), "none", F32),
    "zg": (("lru_gate", "nsa_gate", "fox_gate"), "silu", BF16),
    "zq": (("nsa_q", "nsa_k_slc", "nsa_k_win", "fox_q", "fox_k"), "headnorm", BF16),
    "zv": (("nsa_v_slc", "nsa_v_win", "fox_v"), "none", BF16),
    "zc": (("nsa_k_cmp", "nsa_v_cmp", "fox_f", "nsa_bgate"), "none", F32),
    "zm": (("merge",), "sigmoid", BF16),
}
_PROJ_TILES = {"headnorm": (1024, 1792)}
_PROJ_TILES_DEFAULT = (2048, 1536)


def _gather_cols(w, offs, names, pad_to=None):
    parts = [lax.slice_in_dim(w, offs[n][0], offs[n][0] + offs[n][1], axis=w.ndim - 1) for n in names]
    out = jnp.concatenate(parts, axis=-1) if len(parts) > 1 else parts[0]
    if pad_to is not None and out.shape[-1] % pad_to:
        pad = pad_to - out.shape[-1] % pad_to
        out = jnp.pad(out, [(0, 0)] * (out.ndim - 1) + [(0, pad)])
    return out


def _block_diag(w, pack):
    depth, nb, k, _ = w.shape
    per = pack // k
    w = w.reshape(depth, nb // per, per, k, k)
    eye = jnp.eye(per, dtype=w.dtype)
    return jnp.einsum("dgpij,pq->dgpiqj", w, eye).reshape(depth, nb // per, pack, pack)


def _tile(n, target):
    if n <= target:
        return n
    for unit in (MXU_WIDTH, LANES):
        fits = [t for t in range(unit, target + 1, unit) if n % t == 0]
        if fits:
            return fits[-1]
    return LANES


def kernel(x, p, ln_gain, w_in, b_in, conv_w, conv_b, lru_wa, lru_ba, lru_wx, lru_bx, lru_lambda, cmp_w1, cmp_w2,
           cmp_pos, nsa_q_gain, nsa_k_gain, fox_q_gain, fox_k_gain, w_branch, w_out, w_ple, ple_gain, w_ple_gate):
    batch, seq, d_model = x.shape
    depth = w_in.shape[0]
    bw = conv_w.shape[2]
    heads = bw // HEAD_DIM
    t = batch * seq
    assert seq % NSA_TK == 0 and seq % FOX_T == 0 and seq >= WINDOW + NSA_TQ and seq // SEL_BLOCK <= LANES
    offs = _in_proj_layout(d_model, bw)

    wg, bg = {}, {}
    for name, (members, _, _) in _GROUPS.items():
        wg[name] = _gather_cols(w_in, offs, members, pad_to=LANES).astype(BF16)
        bg[name] = _gather_cols(b_in, offs, members, pad_to=LANES)[:, None, :]
    rep = lambda g, n: jnp.tile(g, (1, n))
    q_scale = HEAD_DIM ** -0.5 * LOG2E
    zq_gain = jnp.concatenate([rep(nsa_q_gain * q_scale, heads), rep(nsa_k_gain, 2 * NSA_KV_HEADS),
                               rep(fox_q_gain * q_scale, heads), rep(fox_k_gain, heads)], axis=1)[:, None, :]
    hb = bw // HEAD_DIM
    cols = {"nsa_q": 0, "k_slc": hb, "k_win": hb + NSA_KV_HEADS, "fox_q": hb + 2 * NSA_KV_HEADS,
            "fox_k": 2 * hb + 2 * NSA_KV_HEADS,
            "v_slc": 0, "v_win": NSA_KV_HEADS, "fox_v": 2 * NSA_KV_HEADS,
            "small": 2 * NSA_KV_HEADS, "bgate_lane": heads,
            "nsa_gate": hb, "fox_gate": 2 * hb}
    wa_bd = _block_diag(lru_wa, LRU_PACK).astype(BF16)
    wx_bd = _block_diag(lru_wx, LRU_PACK).astype(BF16)
    w_branch_b = w_branch.astype(BF16)
    w_out_b = w_out.astype(BF16)
    w_ple_b = w_ple.astype(BF16)
    w_gate_b = w_ple_gate.astype(BF16)
    cmp_w1_b = cmp_w1.astype(BF16)
    cmp_w2_b = cmp_w2.astype(BF16)
    zero_bias = jnp.zeros((1, d_model), F32)

    tm = _tile(t, 1024)
    h = x.reshape(t, d_model)
    p2 = p.reshape(depth, t, p.shape[-1])
    hn = _prenorm(h, ln_gain[0][None], _tile(t, 512))
    for l in range(depth):
        z = {}
        for name, (_, epilogue, dtype) in _GROUPS.items():
            n = wg[name].shape[2]
            tm_target, tn_target = _PROJ_TILES.get(epilogue, _PROJ_TILES_DEFAULT)
            z[name] = _matmul(hn, wg[name], bg[name][l], layer=l, epilogue=epilogue, out_dtype=dtype,
                              tm=_tile(t, tm_target), tn=_tile(n, tn_target),
                              extra=zq_gain[l] if epilogue == "headnorm" else None, name="in_proj_" + name)
        ya = _rglru(z["za"], z["zg"], conv_w[l], conv_b[l][None], wa_bd[l], lru_ba[l][None], wx_bd[l],
                    lru_bx[l][None], lru_lambda[l][None], batch=batch, seq=seq, ts=_tile(seq, 512))
        kvc = _compress(z["zc"], cmp_pos[l], cmp_w1_b[l], cmp_w2_b[l], nsa_k_gain[l][None], batch=batch, seq=seq)
        yb = _nsa(z["zq"], z["zv"], kvc, z["zc"], z["zg"], batch=batch, seq=seq, cols=cols)
        kbias = _forget_prefix(z["zc"], batch=batch, seq=seq, heads=heads, col_block=cols["small"])
        yc = _fox(z["zq"], z["zv"], kbias, z["zg"], batch=batch, seq=seq, heads=heads, cols=cols)
        merged = _merge(ya, yb, yc, w_branch_b, z["zm"], layer=l, tm=_tile(t, 512), tn=d_model)
        h = _matmul(merged, w_out_b, zero_bias, layer=l, epilogue="residual", out_dtype=F32, tm=_tile(t, 512),
                    tn=d_model, extra=h, name="out_proj")
        next_gain = ln_gain[(l + 1) % depth][None]
        h, hn = _ple(h, p2[l], w_ple_b, ple_gain[l][None], w_gate_b, next_gain, layer=l, tm=_tile(t, 512))
    return h.reshape(batch, seq, d_model)
```

```python
import functools

import jax
import jax.numpy as jnp
from jax import lax
from jax.experimental import pallas as pl
from jax.experimental.pallas import tpu as pltpu

F32 = jnp.float32
BF16 = jnp.bfloat16

HEAD_DIM = 128
NORM_EPS = 1e-6
CONV_WIDTH = 4
LRU_C = 8.0
NSA_KV_HEADS = 2
NSA_GROUP = 4
CMP_BLOCK = 32
CMP_STRIDE = 16
SEL_BLOCK = 64
SEL_TOPK = 16
WINDOW = 512
N_BRANCH = 3

V7X_VMEM_BYTES = 64 * 1024 * 1024
VMEM_LIMIT = V7X_VMEM_BYTES - 8 * 1024 * 1024
LANES = 128
SUBLANES = 8
MXU_WIDTH = 256
LOG2E = 1.4426950408889634

MASKED = -1e30
FORCED_SCORE = 1e30
INVALID_SCORE = -1.0


def _params(*semantics):
    return pltpu.CompilerParams(dimension_semantics=semantics, vmem_limit_bytes=VMEM_LIMIT)


def _dot(a, b):
    return jnp.dot(a, b, preferred_element_type=F32)


def _dot_t(a, b):
    return lax.dot_general(a, b, (((1,), (1,)), ((), ())), preferred_element_type=F32)


def _split3(x):
    hi = x.astype(BF16)
    r1 = x - hi.astype(F32)
    mid = r1.astype(BF16)
    lo = (r1 - mid.astype(F32)).astype(BF16)
    return hi, mid, lo


def _sigmoid(x):
    return 0.5 * jnp.tanh(0.5 * x) + 0.5


def _softplus(x):
    return jnp.maximum(x, 0.0) + jnp.log(1.0 + jnp.exp(-jnp.abs(x)))


def _rms(x, gain):
    y = x * lax.rsqrt(jnp.mean(x * x, axis=-1, keepdims=True) + NORM_EPS)
    return y * gain


def _prenorm_kernel(h_ref, g_ref, o_ref):
    o_ref[...] = _rms(h_ref[...], g_ref[...]).astype(BF16)


def _prenorm(h, gain, tm):
    t, d = h.shape
    return pl.pallas_call(
        _prenorm_kernel,
        out_shape=jax.ShapeDtypeStruct((t, d), BF16),
        grid=(t // tm,),
        in_specs=[pl.BlockSpec((tm, d), lambda i: (i, 0)), pl.BlockSpec((1, d), lambda i: (0, 0))],
        out_specs=pl.BlockSpec((tm, d), lambda i: (i, 0)),
        compiler_params=_params("parallel"),
        name="prenorm",
    )(h, gain)


def _matmul_kernel(*refs, epilogue):
    if epilogue in ("headnorm", "residual"):
        x_ref, w_ref, b_ref, e_ref, o_ref = refs
    else:
        x_ref, w_ref, b_ref, o_ref = refs
    tn = o_ref.shape[1]
    step = MXU_WIDTH if tn % MXU_WIDTH == 0 else tn
    x = x_ref[...]
    for c0 in range(0, tn, step):
        cols = slice(c0, c0 + step)
        acc = _dot(x, w_ref[:, cols]) + b_ref[:, cols]
        if epilogue == "silu":
            acc = acc * _sigmoid(acc)
        elif epilogue == "sigmoid":
            acc = _sigmoid(acc)
        elif epilogue == "residual":
            acc = acc + e_ref[:, cols]
        if epilogue == "headnorm":
            for h0 in range(c0, c0 + step, HEAD_DIM):
                head = slice(h0, h0 + HEAD_DIM)
                o_ref[:, head] = _rms(acc[:, h0 - c0:h0 - c0 + HEAD_DIM], e_ref[:, head]).astype(o_ref.dtype)
        else:
            o_ref[:, cols] = acc.astype(o_ref.dtype)


def _matmul(x, w, b, *, layer, epilogue, out_dtype, tm, tn, extra=None, name):
    m, k = x.shape
    n = w.shape[2]
    in_specs = [pl.BlockSpec((tm, k), lambda i, j: (i, 0)),
                pl.BlockSpec((None, k, tn), lambda i, j: (layer, 0, j)),
                pl.BlockSpec((1, tn), lambda i, j: (0, j))]
    args = [x, w, b]
    if epilogue == "headnorm":
        in_specs.append(pl.BlockSpec((1, tn), lambda i, j: (0, j)))
        args.append(extra)
    elif epilogue == "residual":
        in_specs.append(pl.BlockSpec((tm, tn), lambda i, j: (i, j)))
        args.append(extra)
    return pl.pallas_call(
        functools.partial(_matmul_kernel, epilogue=epilogue),
        out_shape=jax.ShapeDtypeStruct((m, n), out_dtype),
        grid=(m // tm, n // tn),
        in_specs=in_specs,
        out_specs=pl.BlockSpec((tm, tn), lambda i, j: (i, j)),
        compiler_params=_params("parallel", "arbitrary"),
        name=name,
    )(*args)


LRU_PACK = 256


def _rglru_kernel(u_ref, g_ref, cw_ref, cb_ref, wa_ref, ba_ref, wx_ref, bx_ref, lam_ref, o_ref,
                  ubuf, abuf, bbuf, hbuf, hcar):
    ts, w = u_ref.shape
    s_idx = pl.program_id(1)

    @pl.when(s_idx == 0)
    def _():
        ubuf[0:8, :] = jnp.zeros((8, w), F32)
        hcar[...] = jnp.zeros((1, w), F32)

    ubuf[8:, :] = u_ref[...]
    uc = cb_ref[...] + cw_ref[CONV_WIDTH - 1:CONV_WIDTH, :] * ubuf[8:, :]
    for d in range(1, CONV_WIDTH):
        uc = uc + cw_ref[CONV_WIDTH - 1 - d:CONV_WIDTH - d, :] * ubuf[pl.ds(8 - d, ts), :]
    ubuf[0:8, :] = ubuf[ts:ts + 8, :]

    ucb = uc.astype(BF16)
    neg_c_softplus = -LRU_C * _softplus(-lam_ref[...])
    for c in range(w // LRU_PACK):
        cols = slice(c * LRU_PACK, (c + 1) * LRU_PACK)
        rec = _sigmoid(_dot(ucb[:, cols], wa_ref[c]) + ba_ref[:, cols])
        inp = _sigmoid(_dot(ucb[:, cols], wx_ref[c]) + bx_ref[:, cols])
        log_a = rec * neg_c_softplus[:, cols]
        a = jnp.exp(log_a)
        th = jnp.tanh(log_a)
        one_minus_a2 = -2.0 * th / (1.0 - th)
        abuf[:, cols] = a
        bbuf[:, cols] = jnp.sqrt(one_minus_a2) * (inp * uc[:, cols])

    def step(t, h):
        h = abuf[pl.ds(t, 1), :] * h + bbuf[pl.ds(t, 1), :]
        hbuf[pl.ds(t, 1), :] = h
        return h

    hcar[...] = lax.fori_loop(0, ts, step, hcar[...], unroll=8)
    o_ref[...] = (hbuf[...] * g_ref[...]).astype(o_ref.dtype)


def _rglru(za, zg, conv_w, conv_b, wa_bd, ba, wx_bd, bx, lam, *, batch, seq, ts):
    w = conv_w.shape[1]
    ns = seq // ts
    row = lambda b, s: (b * ns + s, 0)
    const = lambda b, s: (0, 0)
    return pl.pallas_call(
        _rglru_kernel,
        out_shape=jax.ShapeDtypeStruct((batch * seq, w), BF16),
        grid=(batch, ns),
        in_specs=[pl.BlockSpec((ts, w), row), pl.BlockSpec((ts, w), row),
                  pl.BlockSpec((CONV_WIDTH, w), const), pl.BlockSpec((1, w), const),
                  pl.BlockSpec(wa_bd.shape, lambda b, s: (0, 0, 0)), pl.BlockSpec((1, w), const),
                  pl.BlockSpec(wx_bd.shape, lambda b, s: (0, 0, 0)), pl.BlockSpec((1, w), const),
                  pl.BlockSpec((1, w), const)],
        out_specs=pl.BlockSpec((ts, w), row),
        scratch_shapes=[pltpu.VMEM((ts + 8, w), F32), pltpu.VMEM((ts, w), F32), pltpu.VMEM((ts, w), F32),
                        pltpu.VMEM((ts, w), F32), pltpu.VMEM((1, w), F32)],
        compiler_params=_params("parallel", "arbitrary"),
        name="rglru",
    )(za, zg, conv_w, conv_b, wa_bd, ba, wx_bd, bx, lam)


def _compress_kernel(x_ref, pos_ref, w1_ref, w2_ref, gain_ref, o_ref, xpad):
    seq = x_ref.shape[0]
    n_out = seq // CMP_STRIDE
    j = pl.program_id(1)
    xpad[0:seq, :] = x_ref[...]
    xpad[seq:, :] = jnp.zeros((CMP_BLOCK, HEAD_DIM), F32)
    hidden = jnp.zeros((n_out, w1_ref.shape[2]), F32)
    for r in range(CMP_BLOCK):
        rows = xpad[pl.ds(r, n_out, stride=CMP_STRIDE), :] + pos_ref[0, r:r + 1, :]
        hidden = hidden + _dot(rows.astype(BF16), w1_ref[0, r * HEAD_DIM:(r + 1) * HEAD_DIM, :])
    hidden = hidden * _sigmoid(hidden)
    out = _dot(hidden.astype(BF16), w2_ref[0])
    out = jnp.where(j < NSA_KV_HEADS, _rms(out, gain_ref[...]), out)
    row = lax.broadcasted_iota(jnp.int32, out.shape, 0)
    o_ref[0, 0] = jnp.where(row < n_out - 1, out, 0.0).astype(o_ref.dtype)


def _compress(zc, cmp_pos, cmp_w1, cmp_w2, k_gain, *, batch, seq):
    n_out = seq // CMP_STRIDE
    hid = cmp_w1.shape[2]
    return pl.pallas_call(
        _compress_kernel,
        out_shape=jax.ShapeDtypeStruct((batch, 2 * NSA_KV_HEADS, n_out, HEAD_DIM), BF16),
        grid=(batch, 2 * NSA_KV_HEADS),
        in_specs=[pl.BlockSpec((seq, HEAD_DIM), lambda b, j: (b, j)),
                  pl.BlockSpec((1, CMP_BLOCK, HEAD_DIM), lambda b, j: (j // NSA_KV_HEADS, 0, 0)),
                  pl.BlockSpec((1, CMP_BLOCK * HEAD_DIM, hid), lambda b, j: (j // NSA_KV_HEADS, 0, 0)),
                  pl.BlockSpec((1, hid, HEAD_DIM), lambda b, j: (j // NSA_KV_HEADS, 0, 0)),
                  pl.BlockSpec((1, HEAD_DIM), lambda b, j: (0, 0))],
        out_specs=pl.BlockSpec((1, 1, n_out, HEAD_DIM), lambda b, j: (b, j, 0, 0)),
        scratch_shapes=[pltpu.VMEM((seq + CMP_BLOCK, HEAD_DIM), F32)],
        compiler_params=_params("parallel", "arbitrary"),
        name="compress",
    )(zc, cmp_pos, cmp_w1, cmp_w2, k_gain)


FOX_BIAS_TERMS = 3


def _forget_prefix_kernel(s_ref, o_ref, *, heads):
    seq = s_ref.shape[0]
    f = s_ref[...]
    log_f = jnp.minimum(f, 0.0) - jnp.log(1.0 + jnp.exp(-jnp.abs(f)))
    r = lax.broadcasted_iota(jnp.int32, (LANES, LANES), 0)
    c = lax.broadcasted_iota(jnp.int32, (LANES, LANES), 1)
    lower = jnp.where(c <= r, 1.0, 0.0).astype(BF16)
    pr = lax.broadcasted_iota(jnp.int32, (LANES, heads * LANES), 0)
    pc = lax.broadcasted_iota(jnp.int32, (LANES, heads * LANES), 1)
    place = [jnp.where((pc == pr * LANES + j) & (pr < heads), 1.0, 0.0).astype(BF16) for j in range(FOX_BIAS_TERMS)]
    carry = jnp.zeros((1, LANES), F32)
    for k in range(seq // LANES):
        rows = slice(k * LANES, (k + 1) * LANES)
        hi, mid, lo = _split3(log_f[rows])
        chunk = (_dot(lower, lo) + _dot(lower, mid)) + _dot(lower, hi) + carry
        carry = chunk[LANES - 1:LANES, :]
        terms = _split3(chunk * (-LOG2E))
        out = _dot(terms[0], place[0])
        for j in range(1, FOX_BIAS_TERMS):
            out = out + _dot(terms[j], place[j])
        o_ref[rows, :] = out.astype(o_ref.dtype)


def _forget_prefix(zc, *, batch, seq, heads, col_block):
    return pl.pallas_call(
        functools.partial(_forget_prefix_kernel, heads=heads),
        out_shape=jax.ShapeDtypeStruct((batch * seq, heads * LANES), BF16),
        grid=(batch,),
        in_specs=[pl.BlockSpec((seq, LANES), lambda b: (b, col_block))],
        out_specs=pl.BlockSpec((seq, heads * LANES), lambda b: (b, 0)),
        compiler_params=_params("parallel"),
        name="forget_prefix",
    )(zc)


NSA_TQ = 256
NSA_TK = 512
SEL_SHIFT = SEL_BLOCK.bit_length() - 1


def _biased_exp2(s, bias):
    rows, k = s.shape
    y = (s.reshape(NSA_GROUP, rows // NSA_GROUP, k) + bias[None]).reshape(rows, k)
    e = jnp.exp2(y - jnp.max(y, axis=-1, keepdims=True))
    return e, jnp.sum(e, axis=-1, keepdims=True)


def _top_rows_bias(score, n_top):
    n, t = score.shape
    n_groups = n // SUBLANES
    groups = [score[r * SUBLANES:(r + 1) * SUBLANES] for r in range(n_groups)]
    ranks = [jnp.zeros((SUBLANES, t), F32) for _ in range(n_groups)]
    local = lax.broadcasted_iota(jnp.int32, (SUBLANES, t), 0)
    for i in range(n):
        gi, li = divmod(i, SUBLANES)
        s_i = groups[gi][li:li + 1, :]
        for r in range(n_groups):
            if r > gi:
                ahead = s_i >= groups[r]
            elif r < gi:
                ahead = s_i > groups[r]
            else:
                ahead = (s_i > groups[r]) | ((s_i == groups[r]) & (local > li))
            ranks[r] = ranks[r] + jnp.where(ahead, 1.0, 0.0)
    return jnp.where(jnp.concatenate(ranks, axis=0) < n_top, 0.0, MASKED)


def _nsa_kernel(q_ref, kvc_ref, ks0_ref, ks1_ref, kw0_ref, kw1_ref, vs0_ref, vs1_ref, vw0_ref, vw1_ref,
                small_ref, gate_ref, o_ref, ks_aug, *, n_top, bgate_lane):
    tq = NSA_TQ
    rows = NSA_GROUP * tq
    seq = ks0_ref.shape[0]
    n_cmp_pad = kvc_ref.shape[2]
    n_sel = seq // SEL_BLOCK
    t0 = pl.multiple_of(pl.program_id(1) * tq, tq)
    branch_gate = _sigmoid(small_ref[...])

    @pl.when(pl.program_id(1) == 0)
    def _():
        blk_of_key = lax.broadcasted_iota(jnp.int32, (seq, LANES), 0) >> SEL_SHIFT
        one_hot = jnp.where(blk_of_key == lax.broadcasted_iota(jnp.int32, (seq, LANES), 1), 1.0, 0.0).astype(BF16)
        for g, ks_ref in enumerate((ks0_ref, ks1_ref)):
            ks_aug[g, :, 0:HEAD_DIM] = ks_ref[...]
            ks_aug[g, :, HEAD_DIM:] = one_hot

    tok_col = t0 + lax.broadcasted_iota(jnp.int32, (tq, 1), 0)
    tok_rows = jnp.broadcast_to(tok_col[None], (NSA_GROUP, tq, 1)).reshape(rows, 1)

    for g, (kw_ref, vs_ref, vw_ref) in enumerate(((kw0_ref, vs0_ref, vw0_ref), (kw1_ref, vs1_ref, vw1_ref))):
        q4 = jnp.concatenate(
            [q_ref[:, (g * NSA_GROUP + h) * HEAD_DIM:(g * NSA_GROUP + h + 1) * HEAD_DIM] for h in range(NSA_GROUP)],
            axis=0)

        kc = kvc_ref[0, g]
        vc = kvc_ref[0, NSA_KV_HEADS + g]
        cmp_end = lax.broadcasted_iota(jnp.int32, (1, n_cmp_pad), 1) * CMP_STRIDE + (CMP_BLOCK - 1)
        e_c, sum_c = _biased_exp2(_dot_t(q4, kc), jnp.where(cmp_end <= tok_col, 0.0, MASKED))
        p_c = e_c * jnp.where(tok_rows >= CMP_BLOCK - 1, 1.0 / sum_c, 0.0)
        o_c = _dot(p_c.astype(BF16), vc)

        p_sum = p_c[0:tq]
        for h in range(1, NSA_GROUP):
            p_sum = p_sum + p_c[h * tq:(h + 1) * tq]
        jj = lax.broadcasted_iota(jnp.int32, (n_sel, n_cmp_pad), 0)
        cc = lax.broadcasted_iota(jnp.int32, (n_sel, n_cmp_pad), 1)
        overlap_t = jnp.where((cc * CMP_STRIDE < (jj + 1) * SEL_BLOCK) & (cc * CMP_STRIDE + CMP_BLOCK > jj * SEL_BLOCK)
                              & (cc < n_cmp_pad - 1), 1.0, 0.0).astype(BF16)
        hi, mid, lo = _split3(p_sum)
        imp_t = (_dot_t(overlap_t, lo) + _dot_t(overlap_t, mid)) + _dot_t(overlap_t, hi)
        blk = lax.broadcasted_iota(jnp.int32, (n_sel, tq), 0)
        cur = (t0 + lax.broadcasted_iota(jnp.int32, (n_sel, tq), 1)) >> SEL_SHIFT
        forced = (blk == 0) | (blk == cur) | (blk == cur - 1)
        score = jnp.where(forced, FORCED_SCORE, jnp.where(blk <= cur, imp_t, INVALID_SCORE))
        sel_bias = _top_rows_bias(score, n_top)
        sel_bias = jnp.concatenate([sel_bias, jnp.zeros((LANES - n_sel, tq), F32)], axis=0).T.astype(BF16)
        q_aug = jnp.concatenate([q4, jnp.concatenate([sel_bias] * NSA_GROUP, axis=0)], axis=1)

        def sel_tile(kt, carry, causal):
            m, l, acc = carry
            k0 = pl.multiple_of(kt * NSA_TK, NSA_TK)
            y = _dot_t(q_aug, ks_aug[g, pl.ds(k0, NSA_TK), :])
            if causal:
                kpos = k0 + lax.broadcasted_iota(jnp.int32, (1, NSA_TK), 1)
                y = (y.reshape(NSA_GROUP, tq, NSA_TK) + jnp.where(kpos <= tok_col, 0.0, MASKED)[None]).reshape(
                    rows, NSA_TK)
            m_new = jnp.maximum(m, jnp.max(y, axis=-1, keepdims=True))
            alpha = jnp.exp2(m - m_new)
            e = jnp.exp2(y - m_new)
            l = alpha * l + jnp.sum(e, axis=-1, keepdims=True)
            acc = alpha * acc + _dot(e.astype(BF16), vs_ref[pl.ds(k0, NSA_TK), :])
            return m_new, l, acc

        kd = t0 // NSA_TK
        init = (jnp.full((rows, 1), 3.0 * MASKED, F32), jnp.zeros((rows, 1), F32),
                jnp.zeros((rows, HEAD_DIM), F32))

        def sel_pair(j, carry):
            return sel_tile(2 * j + 1, sel_tile(2 * j, carry, False), False)

        carry = lax.fori_loop(0, kd // 2, sel_pair, init)
        carry = lax.fori_loop(kd - kd % 2, kd, functools.partial(sel_tile, causal=False), carry)
        _, l, acc = sel_tile(kd, carry, True)
        o_s = acc * (1.0 / l)

        w0 = pl.multiple_of(jnp.maximum(t0 - WINDOW, 0), tq)
        wpos = w0 + lax.broadcasted_iota(jnp.int32, (1, WINDOW + tq), 1)
        e_w, sum_w = _biased_exp2(_dot_t(q4, kw_ref[pl.ds(w0, WINDOW + tq), :]),
                                  jnp.where((wpos <= tok_col) & (wpos > tok_col - WINDOW), 0.0, MASKED))
        o_w = _dot(e_w.astype(BF16), vw_ref[pl.ds(w0, WINDOW + tq), :]) * (1.0 / sum_w)

        for h in range(NSA_GROUP):
            head = g * NSA_GROUP + h
            r = slice(h * tq, (h + 1) * tq)
            lane = bgate_lane + head * N_BRANCH
            mix = (branch_gate[:, lane:lane + 1] * o_c[r] + branch_gate[:, lane + 1:lane + 2] * o_s[r]
                   + branch_gate[:, lane + 2:lane + 3] * o_w[r])
            cols = slice(head * HEAD_DIM, (head + 1) * HEAD_DIM)
            o_ref[:, cols] = (mix * gate_ref[:, cols]).astype(o_ref.dtype)


def _nsa(zq, zv, kvc, zc, zg, *, batch, seq, cols):
    nq = seq // NSA_TQ
    width = NSA_KV_HEADS * NSA_GROUP * HEAD_DIM
    n_top = min(SEL_TOPK, seq // SEL_BLOCK)
    row = lambda b, i: b * nq + i
    kv_spec = lambda c: pl.BlockSpec((seq, HEAD_DIM), lambda b, i: (b, c))
    return pl.pallas_call(
        functools.partial(_nsa_kernel, n_top=n_top, bgate_lane=cols["bgate_lane"]),
        out_shape=jax.ShapeDtypeStruct((batch * seq, width), BF16),
        grid=(batch, nq),
        in_specs=[pl.BlockSpec((NSA_TQ, width), lambda b, i: (row(b, i), cols["nsa_q"] * HEAD_DIM // width)),
                  pl.BlockSpec((1,) + kvc.shape[1:], lambda b, i: (b, 0, 0, 0)),
                  kv_spec(cols["k_slc"]), kv_spec(cols["k_slc"] + 1),
                  kv_spec(cols["k_win"]), kv_spec(cols["k_win"] + 1),
                  kv_spec(cols["v_slc"]), kv_spec(cols["v_slc"] + 1),
                  kv_spec(cols["v_win"]), kv_spec(cols["v_win"] + 1),
                  pl.BlockSpec((NSA_TQ, LANES), lambda b, i: (row(b, i), cols["small"])),
                  pl.BlockSpec((NSA_TQ, width), lambda b, i: (row(b, i), cols["nsa_gate"] * HEAD_DIM // width))],
        out_specs=pl.BlockSpec((NSA_TQ, width), lambda b, i: (row(b, i), 0)),
        scratch_shapes=[pltpu.VMEM((NSA_KV_HEADS, seq, HEAD_DIM + LANES), BF16)],
        compiler_params=_params("parallel", "arbitrary"),
        name="sparse_attention",
    )(zq, kvc, zq, zq, zq, zq, zv, zv, zv, zv, zc, zg)


FOX_T = 1024
FOX_TK = 512


def _fox_kernel(q_ref, k_ref, kb_ref, v_ref, gate_ref, o_ref, k_aug):
    t, tk = FOX_T, FOX_TK
    i = pl.program_id(2)
    q0 = i * t

    @pl.when(i == 0)
    def _():
        k_aug[:, 0:HEAD_DIM] = k_ref[...]
        k_aug[:, HEAD_DIM:] = kb_ref[...]

    lane = lax.broadcasted_iota(jnp.int32, (t, LANES), 1)
    q_aug = jnp.concatenate([q_ref[...], jnp.where(lane < FOX_BIAS_TERMS, 1.0, 0.0).astype(BF16)], axis=1)

    def tile(kt, carry, causal=False):
        m, l, acc = carry
        k0 = pl.multiple_of(kt * tk, tk)
        y = _dot_t(q_aug, k_aug[pl.ds(k0, tk), :])
        if causal:
            qpos = q0 + lax.broadcasted_iota(jnp.int32, (t, tk), 0)
            kpos = k0 + lax.broadcasted_iota(jnp.int32, (t, tk), 1)
            y = jnp.where(kpos <= qpos, y, MASKED)
        m_new = jnp.maximum(m, jnp.max(y, axis=-1, keepdims=True))
        alpha = jnp.exp2(m - m_new)
        e = jnp.exp2(y - m_new)
        l = alpha * l + jnp.sum(e, axis=-1, keepdims=True)
        acc = alpha * acc + _dot(e.astype(BF16), v_ref[pl.ds(k0, tk), :])
        return m_new, l, acc

    per_q = t // tk
    n_full = i * per_q

    def tile_group(j, carry):
        for d in range(per_q):
            carry = tile(j * per_q + d, carry)
        return carry

    carry = (jnp.full((t, 1), MASKED, F32), jnp.zeros((t, 1), F32), jnp.zeros((t, HEAD_DIM), F32))
    carry = lax.fori_loop(0, i, tile_group, carry)
    for d in range(per_q):
        carry = tile(n_full + d, carry, causal=True)
    _, l, acc = carry
    o_ref[...] = (acc * (1.0 / l) * gate_ref[...]).astype(o_ref.dtype)


def _fox(zq, zv, kbias, zg, *, batch, seq, heads, cols):
    nq = seq // FOX_T
    row = lambda b, h, i: b * nq + i
    return pl.pallas_call(
        _fox_kernel,
        out_shape=jax.ShapeDtypeStruct((batch * seq, heads * HEAD_DIM), BF16),
        grid=(batch, heads, nq),
        in_specs=[pl.BlockSpec((FOX_T, HEAD_DIM), lambda b, h, i: (row(b, h, i), cols["fox_q"] + h)),
                  pl.BlockSpec((seq, HEAD_DIM), lambda b, h, i: (b, cols["fox_k"] + h)),
                  pl.BlockSpec((seq, LANES), lambda b, h, i: (b, h)),
                  pl.BlockSpec((seq, HEAD_DIM), lambda b, h, i: (b, cols["fox_v"] + h)),
                  pl.BlockSpec((FOX_T, HEAD_DIM), lambda b, h, i: (row(b, h, i), cols["fox_gate"] + h))],
        out_specs=pl.BlockSpec((FOX_T, HEAD_DIM), lambda b, h, i: (row(b, h, i), h)),
        scratch_shapes=[pltpu.VMEM((seq, HEAD_DIM + LANES), BF16)],
        compiler_params=_params("parallel", "parallel", "arbitrary"),
        name="forgetting_attention",
    )(zq, zq, kbias, zv, zg)


def _merge_kernel(ya_ref, yb_ref, yc_ref, w_ref, ga_ref, gb_ref, gc_ref, o_ref):
    tn = o_ref.shape[1]
    step = MXU_WIDTH if tn % MXU_WIDTH == 0 else tn
    ya, yb, yc = ya_ref[...], yb_ref[...], yc_ref[...]
    for c0 in range(0, tn, step):
        cols = slice(c0, c0 + step)
        merged = ga_ref[:, cols] * _dot(ya, w_ref[0, :, cols])
        merged = merged + gb_ref[:, cols] * _dot(yb, w_ref[1, :, cols])
        merged = merged + gc_ref[:, cols] * _dot(yc, w_ref[2, :, cols])
        o_ref[:, cols] = merged.astype(o_ref.dtype)


def _merge(ya, yb, yc, w_branch, zm, *, layer, tm, tn):
    m, k = ya.shape
    n = w_branch.shape[3]
    y_spec = pl.BlockSpec((tm, k), lambda i, j: (i, 0))
    g_spec = lambda br: pl.BlockSpec((tm, tn), lambda i, j: (i, br * (n // tn) + j))
    return pl.pallas_call(
        _merge_kernel,
        out_shape=jax.ShapeDtypeStruct((m, n), BF16),
        grid=(m // tm, n // tn),
        in_specs=[y_spec, y_spec, y_spec, pl.BlockSpec((None, N_BRANCH, k, tn), lambda i, j: (layer, 0, 0, j)),
                  g_spec(0), g_spec(1), g_spec(2)],
        out_specs=pl.BlockSpec((tm, tn), lambda i, j: (i, j)),
        compiler_params=_params("parallel", "arbitrary"),
        name="merge",
    )(ya, yb, yc, w_branch, zm, zm, zm)


def _ple_kernel(h_ref, p_ref, wp_ref, pg_ref, wg_ref, ng_ref, h_out_ref, hn_out_ref):
    h = h_ref[...]
    e = _rms(_dot(p_ref[...].astype(BF16), wp_ref[...]), pg_ref[...])
    h = h + _sigmoid(_dot(h.astype(BF16), wg_ref[...])) * e
    h_out_ref[...] = h
    hn_out_ref[...] = _rms(h, ng_ref[...]).astype(BF16)


def _ple(h, p, w_ple, ple_gain, w_gate, next_gain, *, layer, tm):
    t, d = h.shape
    pd = p.shape[1]
    const = lambda i: (0, 0)
    of_layer = lambda i: (layer, 0, 0)
    return pl.pallas_call(
        _ple_kernel,
        out_shape=(jax.ShapeDtypeStruct((t, d), F32), jax.ShapeDtypeStruct((t, d), BF16)),
        grid=(t // tm,),
        in_specs=[pl.BlockSpec((tm, d), lambda i: (i, 0)), pl.BlockSpec((tm, pd), lambda i: (i, 0)),
                  pl.BlockSpec((None, pd, d), of_layer), pl.BlockSpec((1, d), const),
                  pl.BlockSpec((None, d, d), of_layer), pl.BlockSpec((1, d), const)],
        out_specs=(pl.BlockSpec((tm, d), lambda i: (i, 0)), pl.BlockSpec((tm, d), lambda i: (i, 0))),
        compiler_params=_params("parallel"),
        name="ple_update",
    )(h, p, w_ple, ple_gain, w_gate, next_gain)


def _in_proj_layout(d_model, branch_width):
    kvw = NSA_KV_HEADS * HEAD_DIM
    heads = branch_width // HEAD_DIM
    splits = (("lru_x", branch_width), ("lru_gate", branch_width), ("nsa_q", branch_width),
              ("nsa_k_cmp", kvw), ("nsa_v_cmp", kvw), ("nsa_k_slc", kvw), ("nsa_v_slc", kvw),
              ("nsa_k_win", kvw), ("nsa_v_win", kvw), ("nsa_bgate", heads * N_BRANCH),
              ("nsa_gate", branch_width), ("fox_q", branch_width), ("fox_k", branch_width),
              ("fox_v", branch_width), ("fox_f", heads), ("fox_gate", branch_width),
              ("merge", N_BRANCH * d_model))
    offs, start = {}, 0
    for name, size in splits:
        offs[name] = (start, size)
        start += size
    return offs


_GROUPS = {
    "za": (("lru_x",), "none", F32),
    "zg": (("lru_gate", "nsa_gate", "fox_gate"), "silu", BF16),
    "zq": (("nsa_q", "nsa_k_slc", "nsa_k_win", "fox_q", "fox_k"), "headnorm", BF16),
    "zv": (("nsa_v_slc", "nsa_v_win", "fox_v"), "none", BF16),
    "zc": (("nsa_k_cmp", "nsa_v_cmp", "fox_f", "nsa_bgate"), "none", F32),
    "zm": (("merge",), "sigmoid", BF16),
}
_PROJ_TILES = {"headnorm": (1024, 1792)}
_PROJ_TILES_DEFAULT = (2048, 1536)
ROW_TILE = 512
SCAN_TILE = 512


def _gather_cols(w, offs, names, pad_to=None):
    parts = [lax.slice_in_dim(w, offs[n][0], offs[n][0] + offs[n][1], axis=w.ndim - 1) for n in names]
    out = jnp.concatenate(parts, axis=-1) if len(parts) > 1 else parts[0]
    if pad_to is not None and out.shape[-1] % pad_to:
        pad = pad_to - out.shape[-1] % pad_to
        out = jnp.pad(out, [(0, 0)] * (out.ndim - 1) + [(0, pad)])
    return out


def _block_diag(w, pack):
    depth, nb, k, _ = w.shape
    per = pack // k
    w = w.reshape(depth, nb // per, per, k, k)
    eye = jnp.eye(per, dtype=w.dtype)
    return jnp.einsum("dgpij,pq->dgpiqj", w, eye).reshape(depth, nb // per, pack, pack)


def _tile(n, target):
    if n <= target:
        return n
    for unit in (MXU_WIDTH, LANES):
        fits = [t for t in range(unit, target + 1, unit) if n % t == 0]
        if fits:
            return fits[-1]
    return LANES


def kernel(x, p, ln_gain, w_in, b_in, conv_w, conv_b, lru_wa, lru_ba, lru_wx, lru_bx, lru_lambda, cmp_w1, cmp_w2,
           cmp_pos, nsa_q_gain, nsa_k_gain, fox_q_gain, fox_k_gain, w_branch, w_out, w_ple, ple_gain, w_ple_gate):
    batch, seq, d_model = x.shape
    depth = w_in.shape[0]
    bw = conv_w.shape[2]
    heads = bw // HEAD_DIM
    t = batch * seq
    assert seq % NSA_TK == 0 and seq % FOX_T == 0 and seq >= WINDOW + NSA_TQ and seq // SEL_BLOCK <= LANES
    offs = _in_proj_layout(d_model, bw)

    wg, bg = {}, {}
    for name, (members, _, _) in _GROUPS.items():
        wg[name] = _gather_cols(w_in, offs, members, pad_to=LANES).astype(BF16)
        bg[name] = _gather_cols(b_in, offs, members, pad_to=LANES)[:, None, :]
    rep = lambda g, n: jnp.tile(g, (1, n))
    q_scale = HEAD_DIM ** -0.5 * LOG2E
    zq_gain = jnp.concatenate([rep(nsa_q_gain * q_scale, heads), rep(nsa_k_gain, 2 * NSA_KV_HEADS),
                               rep(fox_q_gain * q_scale, heads), rep(fox_k_gain, heads)], axis=1)[:, None, :]
    hb = bw // HEAD_DIM
    cols = {"nsa_q": 0, "k_slc": hb, "k_win": hb + NSA_KV_HEADS, "fox_q": hb + 2 * NSA_KV_HEADS,
            "fox_k": 2 * hb + 2 * NSA_KV_HEADS,
            "v_slc": 0, "v_win": NSA_KV_HEADS, "fox_v": 2 * NSA_KV_HEADS,
            "small": 2 * NSA_KV_HEADS, "bgate_lane": heads,
            "nsa_gate": hb, "fox_gate": 2 * hb}
    wa_bd = _block_diag(lru_wa, LRU_PACK).astype(BF16)
    wx_bd = _block_diag(lru_wx, LRU_PACK).astype(BF16)
    w_branch_b = w_branch.astype(BF16)
    w_out_b = w_out.astype(BF16)
    w_ple_b = w_ple.astype(BF16)
    w_gate_b = w_ple_gate.astype(BF16)
    cmp_w1_b = cmp_w1.astype(BF16)
    cmp_w2_b = cmp_w2.astype(BF16)
    zero_bias = jnp.zeros((1, d_model), F32)

    tm_row = _tile(t, ROW_TILE)
    h = x.reshape(t, d_model)
    p2 = p.reshape(depth, t, p.shape[-1])
    hn = _prenorm(h, ln_gain[0][None], tm_row)
    for l in range(depth):
        z = {}
        for name, (_, epilogue, dtype) in _GROUPS.items():
            n = wg[name].shape[2]
            tm_target, tn_target = _PROJ_TILES.get(epilogue, _PROJ_TILES_DEFAULT)
            z[name] = _matmul(hn, wg[name], bg[name][l], layer=l, epilogue=epilogue, out_dtype=dtype,
                              tm=_tile(t, tm_target), tn=_tile(n, tn_target),
                              extra=zq_gain[l] if epilogue == "headnorm" else None, name="in_proj_" + name)
        ya = _rglru(z["za"], z["zg"], conv_w[l], conv_b[l][None], wa_bd[l], lru_ba[l][None], wx_bd[l],
                    lru_bx[l][None], lru_lambda[l][None], batch=batch, seq=seq, ts=_tile(seq, SCAN_TILE))
        kvc = _compress(z["zc"], cmp_pos[l], cmp_w1_b[l], cmp_w2_b[l], nsa_k_gain[l][None], batch=batch, seq=seq)
        yb = _nsa(z["zq"], z["zv"], kvc, z["zc"], z["zg"], batch=batch, seq=seq, cols=cols)
        kbias = _forget_prefix(z["zc"], batch=batch, seq=seq, heads=heads, col_block=cols["small"])
        yc = _fox(z["zq"], z["zv"], kbias, z["zg"], batch=batch, seq=seq, heads=heads, cols=cols)
        merged = _merge(ya, yb, yc, w_branch_b, z["zm"], layer=l, tm=tm_row, tn=d_model)
        h = _matmul(merged, w_out_b, zero_bias, layer=l, epilogue="residual", out_dtype=F32, tm=tm_row,
                    tn=d_model, extra=h, name="out_proj")
        next_gain = ln_gain[(l + 1) % depth][None]
        h, hn = _ple(h, p2[l], w_ple_b, ple_gain[l][None], w_gate_b, next_gain, layer=l, tm=tm_row)
    return h.reshape(batch, seq, d_model)
```

```python
import functools

import jax
import jax.numpy as jnp
from jax import lax
from jax.experimental import pallas as pl
from jax.experimental.pallas import tpu as pltpu

F32 = jnp.float32
BF16 = jnp.bfloat16

HEAD_DIM = 128
NORM_EPS = 1e-6
CONV_WIDTH = 4
LRU_C = 8.0
NSA_KV_HEADS = 2
NSA_GROUP = 4
CMP_BLOCK = 32
CMP_STRIDE = 16
SEL_BLOCK = 64
SEL_TOPK = 16
WINDOW = 512
N_BRANCH = 3

V7X_VMEM_BYTES = 64 * 1024 * 1024
VMEM_LIMIT = V7X_VMEM_BYTES - 8 * 1024 * 1024
LANES = 128
SUBLANES = 8
MXU_WIDTH = 256
LOG2E = 1.4426950408889634

MASKED = -1e30
FORCED_SCORE = 1e30
INVALID_SCORE = -1.0


def _params(*semantics):
    return pltpu.CompilerParams(dimension_semantics=semantics, vmem_limit_bytes=VMEM_LIMIT)


def _dot(a, b):
    return jnp.dot(a, b, preferred_element_type=F32)


def _dot_t(a, b):
    return lax.dot_general(a, b, (((1,), (1,)), ((), ())), preferred_element_type=F32)


def _split3(x):
    hi = x.astype(BF16)
    r1 = x - hi.astype(F32)
    mid = r1.astype(BF16)
    lo = (r1 - mid.astype(F32)).astype(BF16)
    return hi, mid, lo


def _sigmoid(x):
    return 0.5 * jnp.tanh(0.5 * x) + 0.5


def _softplus(x):
    return jnp.maximum(x, 0.0) + jnp.log(1.0 + jnp.exp(-jnp.abs(x)))


def _rms(x, gain):
    y = x * lax.rsqrt(jnp.mean(x * x, axis=-1, keepdims=True) + NORM_EPS)
    return y * gain


def _prenorm_kernel(h_ref, g_ref, o_ref):
    o_ref[...] = _rms(h_ref[...], g_ref[...]).astype(BF16)


def _prenorm(h, gain, tm):
    t, d = h.shape
    return pl.pallas_call(
        _prenorm_kernel,
        out_shape=jax.ShapeDtypeStruct((t, d), BF16),
        grid=(t // tm,),
        in_specs=[pl.BlockSpec((tm, d), lambda i: (i, 0)), pl.BlockSpec((1, d), lambda i: (0, 0))],
        out_specs=pl.BlockSpec((tm, d), lambda i: (i, 0)),
        compiler_params=_params("parallel"),
        name="prenorm",
    )(h, gain)


def _matmul_kernel(*refs, epilogue):
    if epilogue in ("headnorm", "residual"):
        x_ref, w_ref, b_ref, e_ref, o_ref = refs
    else:
        x_ref, w_ref, b_ref, o_ref = refs
    tn = o_ref.shape[1]
    step = MXU_WIDTH if tn % MXU_WIDTH == 0 else tn
    x = x_ref[...]
    for c0 in range(0, tn, step):
        cols = slice(c0, c0 + step)
        acc = _dot(x, w_ref[:, cols]) + b_ref[:, cols]
        if epilogue == "silu":
            acc = acc * _sigmoid(acc)
        elif epilogue == "sigmoid":
            acc = _sigmoid(acc)
        elif epilogue == "residual":
            acc = acc + e_ref[:, cols]
        if epilogue == "headnorm":
            for h0 in range(c0, c0 + step, HEAD_DIM):
                head = slice(h0, h0 + HEAD_DIM)
                o_ref[:, head] = _rms(acc[:, h0 - c0:h0 - c0 + HEAD_DIM], e_ref[:, head]).astype(o_ref.dtype)
        else:
            o_ref[:, cols] = acc.astype(o_ref.dtype)


def _matmul(x, w, b, *, layer, epilogue, out_dtype, tm, tn, extra=None, name):
    m, k = x.shape
    n = w.shape[2]
    in_specs = [pl.BlockSpec((tm, k), lambda i, j: (i, 0)),
                pl.BlockSpec((None, k, tn), lambda i, j: (layer, 0, j)),
                pl.BlockSpec((1, tn), lambda i, j: (0, j))]
    args = [x, w, b]
    if epilogue == "headnorm":
        in_specs.append(pl.BlockSpec((1, tn), lambda i, j: (0, j)))
        args.append(extra)
    elif epilogue == "residual":
        in_specs.append(pl.BlockSpec((tm, tn), lambda i, j: (i, j)))
        args.append(extra)
    return pl.pallas_call(
        functools.partial(_matmul_kernel, epilogue=epilogue),
        out_shape=jax.ShapeDtypeStruct((m, n), out_dtype),
        grid=(m // tm, n // tn),
        in_specs=in_specs,
        out_specs=pl.BlockSpec((tm, tn), lambda i, j: (i, j)),
        compiler_params=_params("parallel", "arbitrary"),
        name=name,
    )(*args)


LRU_PACK = 256


def _rglru_kernel(u_ref, g_ref, cw_ref, cb_ref, wa_ref, ba_ref, wx_ref, bx_ref, lam_ref, o_ref,
                  ubuf, abuf, bbuf, hbuf, hcar):
    ts, w = u_ref.shape
    s_idx = pl.program_id(1)

    @pl.when(s_idx == 0)
    def _():
        ubuf[0:8, :] = jnp.zeros((8, w), F32)
        hcar[...] = jnp.zeros((1, w), F32)

    ubuf[8:, :] = u_ref[...]
    uc = cb_ref[...] + cw_ref[CONV_WIDTH - 1:CONV_WIDTH, :] * ubuf[8:, :]
    for d in range(1, CONV_WIDTH):
        uc = uc + cw_ref[CONV_WIDTH - 1 - d:CONV_WIDTH - d, :] * ubuf[pl.ds(8 - d, ts), :]
    ubuf[0:8, :] = ubuf[ts:ts + 8, :]

    ucb = uc.astype(BF16)
    neg_c_softplus = -LRU_C * _softplus(-lam_ref[...])
    for c in range(w // LRU_PACK):
        cols = slice(c * LRU_PACK, (c + 1) * LRU_PACK)
        rec = _sigmoid(_dot(ucb[:, cols], wa_ref[c]) + ba_ref[:, cols])
        inp = _sigmoid(_dot(ucb[:, cols], wx_ref[c]) + bx_ref[:, cols])
        log_a = rec * neg_c_softplus[:, cols]
        a = jnp.exp(log_a)
        th = jnp.tanh(log_a)
        one_minus_a2 = -2.0 * th / (1.0 - th)
        abuf[:, cols] = a
        bbuf[:, cols] = jnp.sqrt(one_minus_a2) * (inp * uc[:, cols])

    def step(t, h):
        h = abuf[pl.ds(t, 1), :] * h + bbuf[pl.ds(t, 1), :]
        hbuf[pl.ds(t, 1), :] = h
        return h

    hcar[...] = lax.fori_loop(0, ts, step, hcar[...], unroll=8)
    o_ref[...] = (hbuf[...] * g_ref[...]).astype(o_ref.dtype)


def _rglru(za, zg, conv_w, conv_b, wa_bd, ba, wx_bd, bx, lam, *, batch, seq, ts):
    w = conv_w.shape[1]
    ns = seq // ts
    row = lambda b, s: (b * ns + s, 0)
    const = lambda b, s: (0, 0)
    return pl.pallas_call(
        _rglru_kernel,
        out_shape=jax.ShapeDtypeStruct((batch * seq, w), BF16),
        grid=(batch, ns),
        in_specs=[pl.BlockSpec((ts, w), row), pl.BlockSpec((ts, w), row),
                  pl.BlockSpec((CONV_WIDTH, w), const), pl.BlockSpec((1, w), const),
                  pl.BlockSpec(wa_bd.shape, lambda b, s: (0, 0, 0)), pl.BlockSpec((1, w), const),
                  pl.BlockSpec(wx_bd.shape, lambda b, s: (0, 0, 0)), pl.BlockSpec((1, w), const),
                  pl.BlockSpec((1, w), const)],
        out_specs=pl.BlockSpec((ts, w), row),
        scratch_shapes=[pltpu.VMEM((ts + 8, w), F32), pltpu.VMEM((ts, w), F32), pltpu.VMEM((ts, w), F32),
                        pltpu.VMEM((ts, w), F32), pltpu.VMEM((1, w), F32)],
        compiler_params=_params("parallel", "arbitrary"),
        name="rglru",
    )(za, zg, conv_w, conv_b, wa_bd, ba, wx_bd, bx, lam)


def _compress_kernel(x_ref, pos_ref, w1_ref, w2_ref, gain_ref, o_ref, xpad):
    seq = x_ref.shape[0]
    n_out = seq // CMP_STRIDE
    j = pl.program_id(1)
    xpad[0:seq, :] = x_ref[...]
    xpad[seq:, :] = jnp.zeros((CMP_BLOCK, HEAD_DIM), F32)
    hidden = jnp.zeros((n_out, w1_ref.shape[2]), F32)
    for r in range(CMP_BLOCK):
        rows = xpad[pl.ds(r, n_out, stride=CMP_STRIDE), :] + pos_ref[0, r:r + 1, :]
        hidden = hidden + _dot(rows.astype(BF16), w1_ref[0, r * HEAD_DIM:(r + 1) * HEAD_DIM, :])
    hidden = hidden * _sigmoid(hidden)
    out = _dot(hidden.astype(BF16), w2_ref[0])
    out = jnp.where(j < NSA_KV_HEADS, _rms(out, gain_ref[...]), out)
    row = lax.broadcasted_iota(jnp.int32, out.shape, 0)
    o_ref[0, 0] = jnp.where(row < n_out - 1, out, 0.0).astype(o_ref.dtype)


def _compress(zc, cmp_pos, cmp_w1, cmp_w2, k_gain, *, batch, seq):
    n_out = seq // CMP_STRIDE
    hid = cmp_w1.shape[2]
    return pl.pallas_call(
        _compress_kernel,
        out_shape=jax.ShapeDtypeStruct((batch, 2 * NSA_KV_HEADS, n_out, HEAD_DIM), BF16),
        grid=(batch, 2 * NSA_KV_HEADS),
        in_specs=[pl.BlockSpec((seq, HEAD_DIM), lambda b, j: (b, j)),
                  pl.BlockSpec((1, CMP_BLOCK, HEAD_DIM), lambda b, j: (j // NSA_KV_HEADS, 0, 0)),
                  pl.BlockSpec((1, CMP_BLOCK * HEAD_DIM, hid), lambda b, j: (j // NSA_KV_HEADS, 0, 0)),
                  pl.BlockSpec((1, hid, HEAD_DIM), lambda b, j: (j // NSA_KV_HEADS, 0, 0)),
                  pl.BlockSpec((1, HEAD_DIM), lambda b, j: (0, 0))],
        out_specs=pl.BlockSpec((1, 1, n_out, HEAD_DIM), lambda b, j: (b, j, 0, 0)),
        scratch_shapes=[pltpu.VMEM((seq + CMP_BLOCK, HEAD_DIM), F32)],
        compiler_params=_params("parallel", "arbitrary"),
        name="compress",
    )(zc, cmp_pos, cmp_w1, cmp_w2, k_gain)


FOX_BIAS_TERMS = 3


def _forget_prefix_kernel(s_ref, o_ref, *, heads):
    seq = s_ref.shape[0]
    f = s_ref[...]
    log_f = jnp.minimum(f, 0.0) - jnp.log(1.0 + jnp.exp(-jnp.abs(f)))
    r = lax.broadcasted_iota(jnp.int32, (LANES, LANES), 0)
    c = lax.broadcasted_iota(jnp.int32, (LANES, LANES), 1)
    lower = jnp.where(c <= r, 1.0, 0.0).astype(BF16)
    pr = lax.broadcasted_iota(jnp.int32, (LANES, heads * LANES), 0)
    pc = lax.broadcasted_iota(jnp.int32, (LANES, heads * LANES), 1)
    place = [jnp.where((pc == pr * LANES + j) & (pr < heads), 1.0, 0.0).astype(BF16) for j in range(FOX_BIAS_TERMS)]
    carry = jnp.zeros((1, LANES), F32)
    for k in range(seq // LANES):
        rows = slice(k * LANES, (k + 1) * LANES)
        hi, mid, lo = _split3(log_f[rows])
        chunk = (_dot(lower, lo) + _dot(lower, mid)) + _dot(lower, hi) + carry
        carry = chunk[LANES - 1:LANES, :]
        terms = _split3(chunk * (-LOG2E))
        out = _dot(terms[0], place[0])
        for j in range(1, FOX_BIAS_TERMS):
            out = out + _dot(terms[j], place[j])
        o_ref[rows, :] = out.astype(o_ref.dtype)


def _forget_prefix(zc, *, batch, seq, heads, col_block):
    return pl.pallas_call(
        functools.partial(_forget_prefix_kernel, heads=heads),
        out_shape=jax.ShapeDtypeStruct((batch * seq, heads * LANES), BF16),
        grid=(batch,),
        in_specs=[pl.BlockSpec((seq, LANES), lambda b: (b, col_block))],
        out_specs=pl.BlockSpec((seq, heads * LANES), lambda b: (b, 0)),
        compiler_params=_params("parallel"),
        name="forget_prefix",
    )(zc)


NSA_TQ = 256
NSA_TK = 512
SEL_SHIFT = SEL_BLOCK.bit_length() - 1


def _biased_exp2(s, bias):
    rows, k = s.shape
    y = (s.reshape(NSA_GROUP, rows // NSA_GROUP, k) + bias[None]).reshape(rows, k)
    e = jnp.exp2(y - jnp.max(y, axis=-1, keepdims=True))
    return e, jnp.sum(e, axis=-1, keepdims=True)


def _top_rows_bias(score, n_top, n_live):
    n, t = score.shape
    n_groups = n // SUBLANES

    def ranked(live, score):
        groups = [score[r * SUBLANES:(r + 1) * SUBLANES] for r in range(live)]
        ranks = [jnp.zeros((SUBLANES, t), F32) for _ in range(live)]
        local = lax.broadcasted_iota(jnp.int32, (SUBLANES, t), 0)
        for i in range(live * SUBLANES):
            gi, li = divmod(i, SUBLANES)
            s_i = groups[gi][li:li + 1, :]
            for r in range(live):
                if r > gi:
                    ahead = s_i >= groups[r]
                elif r < gi:
                    ahead = s_i > groups[r]
                else:
                    ahead = (s_i > groups[r]) | ((s_i == groups[r]) & (local > li))
                ranks[r] = ranks[r] + jnp.where(ahead, 1.0, 0.0)
        bias = [jnp.where(rank < n_top, 0.0, MASKED) for rank in ranks]
        bias += [jnp.full((SUBLANES, t), MASKED, F32)] * (n_groups - live)
        return jnp.concatenate(bias, axis=0)

    return lax.switch(n_live - 1, [functools.partial(ranked, live) for live in range(1, n_groups + 1)], score)


def _nsa_kernel(q_ref, kvc_ref, ks0_ref, ks1_ref, kw0_ref, kw1_ref, vs0_ref, vs1_ref, vw0_ref, vw1_ref,
                small_ref, gate_ref, o_ref, ks_aug, *, n_top, bgate_lane):
    tq = NSA_TQ
    rows = NSA_GROUP * tq
    seq = ks0_ref.shape[0]
    n_cmp_pad = kvc_ref.shape[2]
    n_sel = seq // SEL_BLOCK
    t0 = pl.multiple_of(pl.program_id(1) * tq, tq)
    branch_gate = _sigmoid(small_ref[...])

    @pl.when(pl.program_id(1) == 0)
    def _():
        blk_of_key = lax.broadcasted_iota(jnp.int32, (seq, LANES), 0) >> SEL_SHIFT
        one_hot = jnp.where(blk_of_key == lax.broadcasted_iota(jnp.int32, (seq, LANES), 1), 1.0, 0.0).astype(BF16)
        for g, ks_ref in enumerate((ks0_ref, ks1_ref)):
            ks_aug[g, :, 0:HEAD_DIM] = ks_ref[...]
            ks_aug[g, :, HEAD_DIM:] = one_hot

    tok_col = t0 + lax.broadcasted_iota(jnp.int32, (tq, 1), 0)
    tok_rows = jnp.broadcast_to(tok_col[None], (NSA_GROUP, tq, 1)).reshape(rows, 1)

    for g, (kw_ref, vs_ref, vw_ref) in enumerate(((kw0_ref, vs0_ref, vw0_ref), (kw1_ref, vs1_ref, vw1_ref))):
        q4 = jnp.concatenate(
            [q_ref[:, (g * NSA_GROUP + h) * HEAD_DIM:(g * NSA_GROUP + h + 1) * HEAD_DIM] for h in range(NSA_GROUP)],
            axis=0)

        kc = kvc_ref[0, g]
        vc = kvc_ref[0, NSA_KV_HEADS + g]
        cmp_end = lax.broadcasted_iota(jnp.int32, (1, n_cmp_pad), 1) * CMP_STRIDE + (CMP_BLOCK - 1)
        e_c, sum_c = _biased_exp2(_dot_t(q4, kc), jnp.where(cmp_end <= tok_col, 0.0, MASKED))
        p_c = e_c * jnp.where(tok_rows >= CMP_BLOCK - 1, 1.0 / sum_c, 0.0)
        o_c = _dot(p_c.astype(BF16), vc)

        p_sum = p_c[0:tq]
        for h in range(1, NSA_GROUP):
            p_sum = p_sum + p_c[h * tq:(h + 1) * tq]
        jj = lax.broadcasted_iota(jnp.int32, (n_sel, n_cmp_pad), 0)
        cc = lax.broadcasted_iota(jnp.int32, (n_sel, n_cmp_pad), 1)
        overlap_t = jnp.where((cc * CMP_STRIDE < (jj + 1) * SEL_BLOCK) & (cc * CMP_STRIDE + CMP_BLOCK > jj * SEL_BLOCK)
                              & (cc < n_cmp_pad - 1), 1.0, 0.0).astype(BF16)
        hi, mid, lo = _split3(p_sum)
        imp_t = (_dot_t(overlap_t, lo) + _dot_t(overlap_t, mid)) + _dot_t(overlap_t, hi)
        blk = lax.broadcasted_iota(jnp.int32, (n_sel, tq), 0)
        cur = (t0 + lax.broadcasted_iota(jnp.int32, (n_sel, tq), 1)) >> SEL_SHIFT
        forced = (blk == 0) | (blk == cur) | (blk == cur - 1)
        score = jnp.where(forced, FORCED_SCORE, jnp.where(blk <= cur, imp_t, INVALID_SCORE))
        last_blk = (t0 + tq - 1) >> SEL_SHIFT
        sel_bias = _top_rows_bias(score, n_top, last_blk // SUBLANES + 1)
        sel_bias = jnp.concatenate([sel_bias, jnp.zeros((LANES - n_sel, tq), F32)], axis=0).T.astype(BF16)
        q_aug = jnp.concatenate([q4, jnp.concatenate([sel_bias] * NSA_GROUP, axis=0)], axis=1)

        def sel_tile(kt, carry, causal):
            m, l, acc = carry
            k0 = pl.multiple_of(kt * NSA_TK, NSA_TK)
            y = _dot_t(q_aug, ks_aug[g, pl.ds(k0, NSA_TK), :])
            if causal:
                kpos = k0 + lax.broadcasted_iota(jnp.int32, (1, NSA_TK), 1)
                y = (y.reshape(NSA_GROUP, tq, NSA_TK) + jnp.where(kpos <= tok_col, 0.0, MASKED)[None]).reshape(
                    rows, NSA_TK)
            m_new = jnp.maximum(m, jnp.max(y, axis=-1, keepdims=True))
            alpha = jnp.exp2(m - m_new)
            e = jnp.exp2(y - m_new)
            l = alpha * l + jnp.sum(e, axis=-1, keepdims=True)
            acc = alpha * acc + _dot(e.astype(BF16), vs_ref[pl.ds(k0, NSA_TK), :])
            return m_new, l, acc

        kd = t0 // NSA_TK
        init = (jnp.full((rows, 1), 3.0 * MASKED, F32), jnp.zeros((rows, 1), F32),
                jnp.zeros((rows, HEAD_DIM), F32))

        def sel_pair(j, carry):
            return sel_tile(2 * j + 1, sel_tile(2 * j, carry, False), False)

        carry = lax.fori_loop(0, kd // 2, sel_pair, init)
        carry = lax.fori_loop(kd - kd % 2, kd, functools.partial(sel_tile, causal=False), carry)
        _, l, acc = sel_tile(kd, carry, True)
        o_s = acc * (1.0 / l)

        w0 = pl.multiple_of(jnp.maximum(t0 - WINDOW, 0), tq)
        wpos = w0 + lax.broadcasted_iota(jnp.int32, (1, WINDOW + tq), 1)
        e_w, sum_w = _biased_exp2(_dot_t(q4, kw_ref[pl.ds(w0, WINDOW + tq), :]),
                                  jnp.where((wpos <= tok_col) & (wpos > tok_col - WINDOW), 0.0, MASKED))
        o_w = _dot(e_w.astype(BF16), vw_ref[pl.ds(w0, WINDOW + tq), :]) * (1.0 / sum_w)

        for h in range(NSA_GROUP):
            head = g * NSA_GROUP + h
            r = slice(h * tq, (h + 1) * tq)
            lane = bgate_lane + head * N_BRANCH
            mix = (branch_gate[:, lane:lane + 1] * o_c[r] + branch_gate[:, lane + 1:lane + 2] * o_s[r]
                   + branch_gate[:, lane + 2:lane + 3] * o_w[r])
            cols = slice(head * HEAD_DIM, (head + 1) * HEAD_DIM)
            o_ref[:, cols] = (mix * gate_ref[:, cols]).astype(o_ref.dtype)


def _nsa(zq, zv, kvc, zc, zg, *, batch, seq, cols):
    nq = seq // NSA_TQ
    width = NSA_KV_HEADS * NSA_GROUP * HEAD_DIM
    n_top = min(SEL_TOPK, seq // SEL_BLOCK)
    row = lambda b, i: b * nq + i
    kv_spec = lambda c: pl.BlockSpec((seq, HEAD_DIM), lambda b, i: (b, c))
    return pl.pallas_call(
        functools.partial(_nsa_kernel, n_top=n_top, bgate_lane=cols["bgate_lane"]),
        out_shape=jax.ShapeDtypeStruct((batch * seq, width), BF16),
        grid=(batch, nq),
        in_specs=[pl.BlockSpec((NSA_TQ, width), lambda b, i: (row(b, i), cols["nsa_q"] * HEAD_DIM // width)),
                  pl.BlockSpec((1,) + kvc.shape[1:], lambda b, i: (b, 0, 0, 0)),
                  kv_spec(cols["k_slc"]), kv_spec(cols["k_slc"] + 1),
                  kv_spec(cols["k_win"]), kv_spec(cols["k_win"] + 1),
                  kv_spec(cols["v_slc"]), kv_spec(cols["v_slc"] + 1),
                  kv_spec(cols["v_win"]), kv_spec(cols["v_win"] + 1),
                  pl.BlockSpec((NSA_TQ, LANES), lambda b, i: (row(b, i), cols["small"])),
                  pl.BlockSpec((NSA_TQ, width), lambda b, i: (row(b, i), cols["nsa_gate"] * HEAD_DIM // width))],
        out_specs=pl.BlockSpec((NSA_TQ, width), lambda b, i: (row(b, i), 0)),
        scratch_shapes=[pltpu.VMEM((NSA_KV_HEADS, seq, HEAD_DIM + LANES), BF16)],
        compiler_params=_params("parallel", "arbitrary"),
        name="sparse_attention",
    )(zq, kvc, zq, zq, zq, zq, zv, zv, zv, zv, zc, zg)


FOX_T = 1024
FOX_TK = 512


def _fox_kernel(q_ref, k_ref, kb_ref, v_ref, gate_ref, o_ref, k_aug):
    t, tk = FOX_T, FOX_TK
    i = pl.program_id(2)
    q0 = i * t

    @pl.when(i == 0)
    def _():
        k_aug[:, 0:HEAD_DIM] = k_ref[...]
        k_aug[:, HEAD_DIM:] = kb_ref[...]

    lane = lax.broadcasted_iota(jnp.int32, (t, LANES), 1)
    q_aug = jnp.concatenate([q_ref[...], jnp.where(lane < FOX_BIAS_TERMS, 1.0, 0.0).astype(BF16)], axis=1)

    def tile(kt, carry, causal=False):
        m, l, acc = carry
        k0 = pl.multiple_of(kt * tk, tk)
        y = _dot_t(q_aug, k_aug[pl.ds(k0, tk), :])
        if causal:
            qpos = q0 + lax.broadcasted_iota(jnp.int32, (t, tk), 0)
            kpos = k0 + lax.broadcasted_iota(jnp.int32, (t, tk), 1)
            y = jnp.where(kpos <= qpos, y, MASKED)
        m_new = jnp.maximum(m, jnp.max(y, axis=-1, keepdims=True))
        alpha = jnp.exp2(m - m_new)
        e = jnp.exp2(y - m_new)
        l = alpha * l + jnp.sum(e, axis=-1, keepdims=True)
        acc = alpha * acc + _dot(e.astype(BF16), v_ref[pl.ds(k0, tk), :])
        return m_new, l, acc

    per_q = t // tk
    n_full = i * per_q

    def tile_group(j, carry):
        for d in range(per_q):
            carry = tile(j * per_q + d, carry)
        return carry

    carry = (jnp.full((t, 1), MASKED, F32), jnp.zeros((t, 1), F32), jnp.zeros((t, HEAD_DIM), F32))
    carry = lax.fori_loop(0, i, tile_group, carry)
    for d in range(per_q):
        carry = tile(n_full + d, carry, causal=True)
    _, l, acc = carry
    o_ref[...] = (acc * (1.0 / l) * gate_ref[...]).astype(o_ref.dtype)


def _fox(zq, zv, kbias, zg, *, batch, seq, heads, cols):
    nq = seq // FOX_T
    row = lambda b, h, i: b * nq + i
    return pl.pallas_call(
        _fox_kernel,
        out_shape=jax.ShapeDtypeStruct((batch * seq, heads * HEAD_DIM), BF16),
        grid=(batch, heads, nq),
        in_specs=[pl.BlockSpec((FOX_T, HEAD_DIM), lambda b, h, i: (row(b, h, i), cols["fox_q"] + h)),
                  pl.BlockSpec((seq, HEAD_DIM), lambda b, h, i: (b, cols["fox_k"] + h)),
                  pl.BlockSpec((seq, LANES), lambda b, h, i: (b, h)),
                  pl.BlockSpec((seq, HEAD_DIM), lambda b, h, i: (b, cols["fox_v"] + h)),
                  pl.BlockSpec((FOX_T, HEAD_DIM), lambda b, h, i: (row(b, h, i), cols["fox_gate"] + h))],
        out_specs=pl.BlockSpec((FOX_T, HEAD_DIM), lambda b, h, i: (row(b, h, i), h)),
        scratch_shapes=[pltpu.VMEM((seq, HEAD_DIM + LANES), BF16)],
        compiler_params=_params("parallel", "parallel", "arbitrary"),
        name="forgetting_attention",
    )(zq, zq, kbias, zv, zg)


def _merge_kernel(ya_ref, yb_ref, yc_ref, w_ref, ga_ref, gb_ref, gc_ref, o_ref):
    tn = o_ref.shape[1]
    step = MXU_WIDTH if tn % MXU_WIDTH == 0 else tn
    ya, yb, yc = ya_ref[...], yb_ref[...], yc_ref[...]
    for c0 in range(0, tn, step):
        cols = slice(c0, c0 + step)
        merged = ga_ref[:, cols] * _dot(ya, w_ref[0, :, cols])
        merged = merged + gb_ref[:, cols] * _dot(yb, w_ref[1, :, cols])
        merged = merged + gc_ref[:, cols] * _dot(yc, w_ref[2, :, cols])
        o_ref[:, cols] = merged.astype(o_ref.dtype)


def _merge(ya, yb, yc, w_branch, zm, *, layer, tm, tn):
    m, k = ya.shape
    n = w_branch.shape[3]
    y_spec = pl.BlockSpec((tm, k), lambda i, j: (i, 0))
    g_spec = lambda br: pl.BlockSpec((tm, tn), lambda i, j: (i, br * (n // tn) + j))
    return pl.pallas_call(
        _merge_kernel,
        out_shape=jax.ShapeDtypeStruct((m, n), BF16),
        grid=(m // tm, n // tn),
        in_specs=[y_spec, y_spec, y_spec, pl.BlockSpec((None, N_BRANCH, k, tn), lambda i, j: (layer, 0, 0, j)),
                  g_spec(0), g_spec(1), g_spec(2)],
        out_specs=pl.BlockSpec((tm, tn), lambda i, j: (i, j)),
        compiler_params=_params("parallel", "arbitrary"),
        name="merge",
    )(ya, yb, yc, w_branch, zm, zm, zm)


def _ple_kernel(h_ref, p_ref, wp_ref, pg_ref, wg_ref, ng_ref, h_out_ref, hn_out_ref):
    h = h_ref[...]
    e = _rms(_dot(p_ref[...].astype(BF16), wp_ref[...]), pg_ref[...])
    h = h + _sigmoid(_dot(h.astype(BF16), wg_ref[...])) * e
    h_out_ref[...] = h
    hn_out_ref[...] = _rms(h, ng_ref[...]).astype(BF16)


def _ple(h, p, w_ple, ple_gain, w_gate, next_gain, *, layer, tm):
    t, d = h.shape
    pd = p.shape[1]
    const = lambda i: (0, 0)
    of_layer = lambda i: (layer, 0, 0)
    return pl.pallas_call(
        _ple_kernel,
        out_shape=(jax.ShapeDtypeStruct((t, d), F32), jax.ShapeDtypeStruct((t, d), BF16)),
        grid=(t // tm,),
        in_specs=[pl.BlockSpec((tm, d), lambda i: (i, 0)), pl.BlockSpec((tm, pd), lambda i: (i, 0)),
                  pl.BlockSpec((None, pd, d), of_layer), pl.BlockSpec((1, d), const),
                  pl.BlockSpec((None, d, d), of_layer), pl.BlockSpec((1, d), const)],
        out_specs=(pl.BlockSpec((tm, d), lambda i: (i, 0)), pl.BlockSpec((tm, d), lambda i: (i, 0))),
        compiler_params=_params("parallel"),
        name="ple_update",
    )(h, p, w_ple, ple_gain, w_gate, next_gain)


def _in_proj_layout(d_model, branch_width):
    kvw = NSA_KV_HEADS * HEAD_DIM
    heads = branch_width // HEAD_DIM
    splits = (("lru_x", branch_width), ("lru_gate", branch_width), ("nsa_q", branch_width),
              ("nsa_k_cmp", kvw), ("nsa_v_cmp", kvw), ("nsa_k_slc", kvw), ("nsa_v_slc", kvw),
              ("nsa_k_win", kvw), ("nsa_v_win", kvw), ("nsa_bgate", heads * N_BRANCH),
              ("nsa_gate", branch_width), ("fox_q", branch_width), ("fox_k", branch_width),
              ("fox_v", branch_width), ("fox_f", heads), ("fox_gate", branch_width),
              ("merge", N_BRANCH * d_model))
    offs, start = {}, 0
    for name, size in splits:
        offs[name] = (start, size)
        start += size
    return offs


_GROUPS = {
    "za": (("lru_x",), "none", F32),
    "zg": (("lru_gate", "nsa_gate", "fox_gate"), "silu", BF16),
    "zq": (("nsa_q", "nsa_k_slc", "nsa_k_win", "fox_q", "fox_k"), "headnorm", BF16),
    "zv": (("nsa_v_slc", "nsa_v_win", "fox_v"), "none", BF16),
    "zc": (("nsa_k_cmp", "nsa_v_cmp", "fox_f", "nsa_bgate"), "none", F32),
    "zm": (("merge",), "sigmoid", BF16),
}
_PROJ_TILES = {"headnorm": (1024, 1792)}
_PROJ_TILES_DEFAULT = (2048, 1536)
ROW_TILE = 512
SCAN_TILE = 512


def _gather_cols(w, offs, names, pad_to=None):
    parts = [lax.slice_in_dim(w, offs[n][0], offs[n][0] + offs[n][1], axis=w.ndim - 1) for n in names]
    out = jnp.concatenate(parts, axis=-1) if len(parts) > 1 else parts[0]
    if pad_to is not None and out.shape[-1] % pad_to:
        pad = pad_to - out.shape[-1] % pad_to
        out = jnp.pad(out, [(0, 0)] * (out.ndim - 1) + [(0, pad)])
    return out


def _block_diag(w, pack):
    depth, nb, k, _ = w.shape
    per = pack // k
    w = w.reshape(depth, nb // per, per, k, k)
    eye = jnp.eye(per, dtype=w.dtype)
    return jnp.einsum("dgpij,pq->dgpiqj", w, eye).reshape(depth, nb // per, pack, pack)


def _tile(n, target):
    if n <= target:
        return n
    for unit in (MXU_WIDTH, LANES):
        fits = [t for t in range(unit, target + 1, unit) if n % t == 0]
        if fits:
            return fits[-1]
    return LANES


def kernel(x, p, ln_gain, w_in, b_in, conv_w, conv_b, lru_wa, lru_ba, lru_wx, lru_bx, lru_lambda, cmp_w1, cmp_w2,
           cmp_pos, nsa_q_gain, nsa_k_gain, fox_q_gain, fox_k_gain, w_branch, w_out, w_ple, ple_gain, w_ple_gate):
    batch, seq, d_model = x.shape
    depth = w_in.shape[0]
    bw = conv_w.shape[2]
    heads = bw // HEAD_DIM
    t = batch * seq
    assert seq % NSA_TK == 0 and seq % FOX_T == 0 and seq >= WINDOW + NSA_TQ and seq // SEL_BLOCK <= LANES
    offs = _in_proj_layout(d_model, bw)

    wg, bg = {}, {}
    for name, (members, _, _) in _GROUPS.items():
        wg[name] = _gather_cols(w_in, offs, members, pad_to=LANES).astype(BF16)
        bg[name] = _gather_cols(b_in, offs, members, pad_to=LANES)[:, None, :]
    rep = lambda g, n: jnp.tile(g, (1, n))
    q_scale = HEAD_DIM ** -0.5 * LOG2E
    zq_gain = jnp.concatenate([rep(nsa_q_gain * q_scale, heads), rep(nsa_k_gain, 2 * NSA_KV_HEADS),
                               rep(fox_q_gain * q_scale, heads), rep(fox_k_gain, heads)], axis=1)[:, None, :]
    hb = bw // HEAD_DIM
    cols = {"nsa_q": 0, "k_slc": hb, "k_win": hb + NSA_KV_HEADS, "fox_q": hb + 2 * NSA_KV_HEADS,
            "fox_k": 2 * hb + 2 * NSA_KV_HEADS,
            "v_slc": 0, "v_win": NSA_KV_HEADS, "fox_v": 2 * NSA_KV_HEADS,
            "small": 2 * NSA_KV_HEADS, "bgate_lane": heads,
            "nsa_gate": hb, "fox_gate": 2 * hb}
    wa_bd = _block_diag(lru_wa, LRU_PACK).astype(BF16)
    wx_bd = _block_diag(lru_wx, LRU_PACK).astype(BF16)
    w_branch_b = w_branch.astype(BF16)
    w_out_b = w_out.astype(BF16)
    w_ple_b = w_ple.astype(BF16)
    w_gate_b = w_ple_gate.astype(BF16)
    cmp_w1_b = cmp_w1.astype(BF16)
    cmp_w2_b = cmp_w2.astype(BF16)
    zero_bias = jnp.zeros((1, d_model), F32)

    tm_row = _tile(t, ROW_TILE)
    h = x.reshape(t, d_model)
    p2 = p.reshape(depth, t, p.shape[-1])
    hn = _prenorm(h, ln_gain[0][None], tm_row)
    for l in range(depth):
        z = {}
        for name, (_, epilogue, dtype) in _GROUPS.items():
            n = wg[name].shape[2]
            tm_target, tn_target = _PROJ_TILES.get(epilogue, _PROJ_TILES_DEFAULT)
            z[name] = _matmul(hn, wg[name], bg[name][l], layer=l, epilogue=epilogue, out_dtype=dtype,
                              tm=_tile(t, tm_target), tn=_tile(n, tn_target),
                              extra=zq_gain[l] if epilogue == "headnorm" else None, name="in_proj_" + name)
        ya = _rglru(z["za"], z["zg"], conv_w[l], conv_b[l][None], wa_bd[l], lru_ba[l][None], wx_bd[l],
                    lru_bx[l][None], lru_lambda[l][None], batch=batch, seq=seq, ts=_tile(seq, SCAN_TILE))
        kvc = _compress(z["zc"], cmp_pos[l], cmp_w1_b[l], cmp_w2_b[l], nsa_k_gain[l][None], batch=batch, seq=seq)
        yb = _nsa(z["zq"], z["zv"], kvc, z["zc"], z["zg"], batch=batch, seq=seq, cols=cols)
        kbias = _forget_prefix(z["zc"], batch=batch, seq=seq, heads=heads, col_block=cols["small"])
        yc = _fox(z["zq"], z["zv"], kbias, z["zg"], batch=batch, seq=seq, heads=heads, cols=cols)
        merged = _merge(ya, yb, yc, w_branch_b, z["zm"], layer=l, tm=tm_row, tn=d_model)
        h = _matmul(merged, w_out_b, zero_bias, layer=l, epilogue="residual", out_dtype=F32, tm=tm_row,
                    tn=d_model, extra=h, name="out_proj")
        next_gain = ln_gain[(l + 1) % depth][None]
        h, hn = _ple(h, p2[l], w_ple_b, ple_gain[l][None], w_gate_b, next_gain, layer=l, tm=tm_row)
    return h.reshape(batch, seq, d_model)
```
